```python
import math
import jax, jax.numpy as jnp
from jax import lax
import numpy as np

D_MODEL = 2048
BATCH = 2
SEQ = 4096
DEPTH = 4
DEC_BATCH = 32
DEC_SEQ = 1
PAST_LEN = 16384
PAGE_SIZE = 128

N_A_LAYERS = DEPTH // 2
N_B_LAYERS = DEPTH - N_A_LAYERS
A_HEADS = 16
A_HEAD_DIM = D_MODEL // A_HEADS
A_BRANCHES = ((128, 1), (512, 4), (2048, 16))
A_MAX_WINDOW = max(w for w, _ in A_BRANCHES)
B_HEADS = 32
B_HEAD_DIM = D_MODEL // B_HEADS
B_KV_HEADS = 4
B_GROUP = B_HEADS // B_KV_HEADS
B_WINDOW = 128
ROPE_THETA = 500000.0
ROT_FRACTION = 4
N_EXPERTS = 16
N_GROUPS = 4
EXPERTS_PER_GROUP = N_EXPERTS // N_GROUPS
TOP_K = 2
D_EXPERT = D_MODEL // 2
BLOCK = 128
NORM_EPS = 1e-6
NEG_INF = -1e30

kernel_name = "yoco_dilated_swa_sink_moe_step"

F32 = jnp.float32


def _modulation(c, w, b, n_chunks):
    mod = jax.nn.silu(c) @ w + b
    return jnp.split(mod[:, None, :], n_chunks, axis=-1)


def _adaln(x, shift, scale):
    xf = x.astype(F32)
    xf = xf * lax.rsqrt(jnp.mean(xf * xf, axis=-1, keepdims=True) + NORM_EPS)
    return (xf * (1 + scale.astype(F32)) + shift.astype(F32)).astype(x.dtype)


def _head_rmsnorm(x, gain):
    xf = x.astype(F32)
    xf = xf * lax.rsqrt(jnp.mean(xf * xf, axis=-1, keepdims=True) + NORM_EPS)
    return (xf * gain.astype(F32)).astype(x.dtype)


def _rope_partial(x, pos):
    d = x.shape[-1]
    rot = d // ROT_FRACTION
    half = rot // 2
    inv = jnp.exp(jnp.arange(half, dtype=F32) * (-math.log(ROPE_THETA) / half))
    ang = pos.astype(F32)[:, None] * inv[None, :]
    cos = jnp.cos(ang)[:, None, :]
    sin = jnp.sin(ang)[:, None, :]
    xf = x.astype(F32)
    x1, x2, rest = xf[..., :half], xf[..., half:rot], xf[..., rot:]
    return jnp.concatenate([x1 * cos - x2 * sin, x2 * cos + x1 * sin, rest], axis=-1).astype(x.dtype)


def _softmax_weights(s, valid, sink=None):
    s = jnp.where(valid, s, NEG_INF)
    m = jnp.max(s, axis=-1, keepdims=True)
    if sink is not None:
        m = jnp.maximum(m, sink)
    p = jnp.exp(s - m)
    den = jnp.sum(p, axis=-1, keepdims=True)
    if sink is not None:
        den = den + jnp.exp(sink - m)
    return p / den, (m + jnp.log(den))[..., 0]


def _banded_attention(q, k, v, window, sinks=None):
    assert window <= BLOCK
    n, L, hk, g, dh = q.shape
    nb = -(-L // BLOCK)
    pad = nb * BLOCK - L
    qb = jnp.pad(q, ((0, 0), (0, pad), (0, 0), (0, 0), (0, 0))).reshape(n, nb, BLOCK, hk, g, dh)

    def windows(a):
        a = jnp.pad(a, ((0, 0), (BLOCK, pad), (0, 0), (0, 0))).reshape(n, nb + 1, BLOCK, hk, dh)
        return jnp.concatenate([a[:, :-1], a[:, 1:]], axis=2)

    kw, vw = windows(k), windows(v)
    s = jnp.einsum('nbqhgd,nbkhd->nbhgqk', qb, kw, preferred_element_type=F32) * dh ** -0.5
    qi = jnp.arange(BLOCK)[:, None] + BLOCK
    kj = jnp.arange(2 * BLOCK)[None, :]
    dist = qi - kj
    band = (dist >= 0) & (dist <= window)
    key_pos = jnp.arange(nb)[:, None] * BLOCK + kj - BLOCK
    valid = (band[None] & (key_pos >= 0)[:, None, :])[None, :, None, None]
    sink = None if sinks is None else sinks.astype(F32)[:, :, None, None]
    p, lse = _softmax_weights(s, valid, sink)
    out = jnp.einsum('nbhgqk,nbkhd->nbqhgd', p, vw.astype(F32)).reshape(n, nb * BLOCK, hk, g, dh)[:, :L]
    lse = lse.transpose(0, 1, 4, 2, 3).reshape(n, nb * BLOCK, hk, g)[:, :L]
    return out, lse


def _combine_branches(outs, lses):
    alpha = jax.nn.softmax(jnp.stack(lses), axis=0)
    return jnp.sum(alpha[..., None] * jnp.stack(outs), axis=0)


def _dilated_attention_prompt(q, k, v):
    n, t, h, d = q.shape
    outs, lses = [], []
    for w, r in A_BRANCHES:
        L = t // r

        def to_sub(a):
            return a.reshape(n, L, r, h, d).transpose(0, 2, 1, 3, 4).reshape(n * r, L, h, d)

        o, lse = _banded_attention(to_sub(q)[:, :, :, None, :], to_sub(k), to_sub(v), w // r)
        outs.append(o[:, :, :, 0].reshape(n, r, L, h, d).transpose(0, 2, 1, 3, 4).reshape(n, t, h, d))
        lses.append(lse[..., 0].reshape(n, r, L, h).transpose(0, 2, 1, 3).reshape(n, t, h))
    return _combine_branches(outs, lses)


def _dilated_attention_step(q, k_all, v_all):
    n, s_len, h, d = q.shape
    m_len = k_all.shape[1]
    outs, lses = [], []
    for w, r in A_BRANCHES:
        j = jnp.arange(w // r + 1)
        idx = (m_len - s_len) + jnp.arange(s_len)[:, None] - j[None, :] * r
        valid = idx >= 0
        idx = jnp.maximum(idx, 0)
        kg, vg = k_all[:, idx], v_all[:, idx]
        s = jnp.einsum('nshd,nsjhd->nhsj', q, kg, preferred_element_type=F32) * d ** -0.5
        p, lse = _softmax_weights(s, valid[None, None])
        outs.append(jnp.einsum('nhsj,nsjhd->nshd', p, vg.astype(F32)))
        lses.append(lse.transpose(0, 2, 1))
    return _combine_branches(outs, lses)


def _swa_sink_step(q, k_all, v_all, sinks):
    s_len, m_len = q.shape[1], k_all.shape[1]
    s = jnp.einsum('nshgd,nmhd->nhgsm', q, k_all, preferred_element_type=F32) * B_HEAD_DIM ** -0.5
    dist = (m_len - s_len + jnp.arange(s_len))[:, None] - jnp.arange(m_len)[None, :]
    valid = (dist >= 0) & (dist <= B_WINDOW)
    p, _ = _softmax_weights(s, valid, sinks.astype(F32)[:, :, None, None])
    return jnp.einsum('nhgsm,nmhd->nshgd', p, v_all.astype(F32))


def _shared_kv(x, c, pos, kv_w, kv_k_gain, kv_mod_w, kv_mod_b):
    n, t, _ = x.shape
    shift, scale = _modulation(c, kv_mod_w, kv_mod_b, 2)
    k, v = jnp.split(_adaln(x, shift, scale) @ kv_w, 2, axis=-1)
    k = _rope_partial(_head_rmsnorm(k.reshape(n, t, B_KV_HEADS, B_HEAD_DIM), kv_k_gain), pos)
    return k, v.reshape(n, t, B_KV_HEADS, B_HEAD_DIM)


def _moe(h, router_w, router_bias, w_gate, w_up, w_down):
    n, t, d = h.shape
    hf = h.reshape(n * t, d)
    scores = jax.nn.sigmoid((hf @ router_w).astype(F32))
    sel = (scores + router_bias.astype(F32)).reshape(-1, N_GROUPS, EXPERTS_PER_GROUP)
    group_score = jnp.sum(lax.top_k(sel, TOP_K)[0], axis=-1)
    g_idx = jnp.argmax(group_score, axis=-1)
    in_group = jnp.take_along_axis(sel, g_idx[:, None, None], axis=1)[:, 0]
    _, local = lax.top_k(in_group, TOP_K)
    e_idx = g_idx[:, None] * EXPERTS_PER_GROUP + local
    wsel = jnp.take_along_axis(scores, e_idx, axis=-1)
    wsel = wsel / jnp.sum(wsel, axis=-1, keepdims=True)
    gates = jnp.sum(jax.nn.one_hot(e_idx, N_EXPERTS, dtype=F32) * wsel[..., None], axis=1)
    a = jnp.einsum('md,edf->mef', hf, w_gate)
    u = jnp.einsum('md,edf->mef', hf, w_up)
    hidden = jax.nn.silu(a) * u * gates[..., None].astype(h.dtype)
    return jnp.einsum('mef,efd->md', hidden, w_down).reshape(n, t, d)


def setup_inputs(seed: int = 0) -> dict:
    key = jax.random.key(seed)
    ks = jax.random.split(key, 32)
    D = D_MODEL
    a_buf = min(A_MAX_WINDOW, PAST_LEN)
    b_buf = min(B_WINDOW, PAST_LEN)

    def nrm(k, shape, scale):
        return jax.random.normal(k, shape, F32) * scale

    return {
        "x_prompt": nrm(ks[0], (BATCH, SEQ, D), 1.0),
        "x_sample": nrm(ks[1], (DEC_BATCH, DEC_SEQ, D), 1.0),
        "cache_a_k": nrm(ks[2], (N_A_LAYERS, DEC_BATCH, a_buf, A_HEADS, A_HEAD_DIM), 1.0),
        "cache_a_v": nrm(ks[3], (N_A_LAYERS, DEC_BATCH, a_buf, A_HEADS, A_HEAD_DIM), 1.0),
        "cache_b_k": nrm(ks[4], (DEC_BATCH, b_buf, B_KV_HEADS, B_HEAD_DIM), 1.0),
        "cache_b_v": nrm(ks[5], (DEC_BATCH, b_buf, B_KV_HEADS, B_HEAD_DIM), 1.0),
        "c_prompt": nrm(ks[6], (BATCH, D), 1.0),
        "c_sample": nrm(ks[7], (DEC_BATCH, D), 1.0),
        "a_w_qkv": nrm(ks[8], (N_A_LAYERS, D, 3 * A_HEADS * A_HEAD_DIM), D ** -0.5),
        "a_q_gain": 1.0 + nrm(ks[9], (N_A_LAYERS, A_HEAD_DIM), 0.02),
        "a_k_gain": 1.0 + nrm(ks[10], (N_A_LAYERS, A_HEAD_DIM), 0.02),
        "a_w_o": nrm(ks[11], (N_A_LAYERS, A_HEADS * A_HEAD_DIM, D), (A_HEADS * A_HEAD_DIM) ** -0.5),
        "b_w_q": nrm(ks[12], (N_B_LAYERS, D, B_HEADS * B_HEAD_DIM), D ** -0.5),
        "b_q_gain": 1.0 + nrm(ks[13], (N_B_LAYERS, B_HEAD_DIM), 0.02),
        "b_sinks": nrm(ks[14], (N_B_LAYERS, B_HEADS), 1.0),
        "b_w_o": nrm(ks[15], (N_B_LAYERS, B_HEADS * B_HEAD_DIM, D), (B_HEADS * B_HEAD_DIM) ** -0.5),
        "kv_w": nrm(ks[16], (D, 2 * B_KV_HEADS * B_HEAD_DIM), D ** -0.5),
        "kv_k_gain": 1.0 + nrm(ks[17], (B_HEAD_DIM,), 0.02),
        "kv_mod_w": nrm(ks[18], (D, 2 * D), 0.5 * D ** -0.5),
        "kv_mod_b": nrm(ks[19], (2 * D,), 0.02),
        "mod_w": nrm(ks[20], (DEPTH, D, 6 * D), 0.5 * D ** -0.5),
        "mod_b": nrm(ks[21], (DEPTH, 6 * D), 0.02),
        "router_w": nrm(ks[22], (D, N_EXPERTS), D ** -0.5),
        "router_bias": nrm(ks[23], (N_EXPERTS,), 0.01),
        "moe_w_gate": nrm(ks[24], (DEPTH, N_EXPERTS, D, D_EXPERT), D ** -0.5),
        "moe_w_up": nrm(ks[25], (DEPTH, N_EXPERTS, D, D_EXPERT), D ** -0.5),
        "moe_w_down": nrm(ks[26], (DEPTH, N_EXPERTS, D_EXPERT, D), D_EXPERT ** -0.5),
    }


def reference(x_prompt, x_sample, cache_a_k, cache_a_v, cache_b_k, cache_b_v, c_prompt, c_sample,
              a_w_qkv, a_q_gain, a_k_gain, a_w_o, b_w_q, b_q_gain, b_sinks, b_w_o,
              kv_w, kv_k_gain, kv_mod_w, kv_mod_b, mod_w, mod_b, router_w, router_bias,
              moe_w_gate, moe_w_up, moe_w_down):

    def run(x, c, pos, a_k_cache, a_v_cache, b_k_cache, b_v_cache):
        n, t, _ = x.shape
        a_k_rows, a_v_rows = [], []
        kb = vb = b_k_new = b_v_new = None
        for l in range(DEPTH):
            shift1, scale1, gate1, shift2, scale2, gate2 = _modulation(c, mod_w[l], mod_b[l], 6)
            h = _adaln(x, shift1, scale1)
            if l < N_A_LAYERS:
                q, k, v = jnp.split(h @ a_w_qkv[l], 3, axis=-1)
                q = _rope_partial(_head_rmsnorm(q.reshape(n, t, A_HEADS, A_HEAD_DIM), a_q_gain[l]), pos)
                k = _rope_partial(_head_rmsnorm(k.reshape(n, t, A_HEADS, A_HEAD_DIM), a_k_gain[l]), pos)
                v = v.reshape(n, t, A_HEADS, A_HEAD_DIM)
                if a_k_cache is None:
                    o = _dilated_attention_prompt(q, k, v)
                    k_all, v_all = k, v
                else:
                    k_all = jnp.concatenate([a_k_cache[l], k], axis=1)
                    v_all = jnp.concatenate([a_v_cache[l], v], axis=1)
                    o = _dilated_attention_step(q, k_all, v_all)
                keep = min(A_MAX_WINDOW, k_all.shape[1])
                a_k_rows.append(k_all[:, -keep:])
                a_v_rows.append(v_all[:, -keep:])
                o = o.reshape(n, t, -1).astype(x.dtype) @ a_w_o[l]
            else:
                if l == N_A_LAYERS:
                    kb, vb = _shared_kv(x, c, pos, kv_w, kv_k_gain, kv_mod_w, kv_mod_b)
                    if b_k_cache is not None:
                        kb = jnp.concatenate([b_k_cache, kb], axis=1)
                        vb = jnp.concatenate([b_v_cache, vb], axis=1)
                    keep_b = min(B_WINDOW, kb.shape[1])
                    b_k_new, b_v_new = kb[:, -keep_b:], vb[:, -keep_b:]
                j = l - N_A_LAYERS
                q = _rope_partial(_head_rmsnorm((h @ b_w_q[j]).reshape(n, t, B_HEADS, B_HEAD_DIM), b_q_gain[j]), pos)
                q = q.reshape(n, t, B_KV_HEADS, B_GROUP, B_HEAD_DIM)
                sinks = b_sinks[j].reshape(B_KV_HEADS, B_GROUP)
                if b_k_cache is None:
                    o, _ = _banded_attention(q, kb, vb, B_WINDOW, sinks)
                else:
                    o = _swa_sink_step(q, kb, vb, sinks)
                o = o.reshape(n, t, -1).astype(x.dtype) @ b_w_o[j]
            x = x + gate1 * o
            h = _adaln(x, shift2, scale2)
            x = x + gate2 * _moe(h, router_w, router_bias, moe_w_gate[l], moe_w_up[l], moe_w_down[l])
        return x, jnp.stack(a_k_rows), jnp.stack(a_v_rows), b_k_new, b_v_new

    pos_p = jnp.arange(x_prompt.shape[1], dtype=jnp.int32)
    pos_s = PAST_LEN + jnp.arange(x_sample.shape[1], dtype=jnp.int32)
    y_prompt, ak_p, av_p, bk_p, bv_p = run(x_prompt, c_prompt, pos_p, None, None, None, None)
    y_sample, ak_s, av_s, bk_s, bv_s = run(x_sample, c_sample, pos_s, cache_a_k, cache_a_v, cache_b_k, cache_b_v)
    return (y_prompt, y_sample, ak_p, av_p, bk_p, bv_p, ak_s, av_s, bk_s, bv_s)
```

```python
import functools
import math

import jax
import jax.numpy as jnp
from jax import lax
from jax.experimental import pallas as pl
from jax.experimental.pallas import tpu as pltpu

F32 = jnp.float32
BF16 = jnp.bfloat16

D_MODEL = 2048
DEPTH = 4
N_A_LAYERS = DEPTH // 2
PAST_LEN = 16384
A_HEADS = 16
A_HEAD_DIM = 128
A_BRANCH_DILATIONS = (1, 4, 16)
B_HEADS = 32
B_HEAD_DIM = 64
B_KV_HEADS = 4
B_GROUP = B_HEADS // B_KV_HEADS
ROPE_THETA = 500000.0
N_EXPERTS = 16
N_GROUPS = 4
EXPERTS_PER_GROUP = N_EXPERTS // N_GROUPS
D_EXPERT = D_MODEL // 2
BLOCK = 128
LANES = 128
NORM_EPS = 1e-6
NEG_INF = -1e30

A_QBLOCK = 2048
MOE_TM = 512
MOE_TF = 256
VMEM_LIMIT = 56 * 1024 * 1024


def _cparams(n_axes, vmem=VMEM_LIMIT):
    return pltpu.CompilerParams(dimension_semantics=("arbitrary",) * n_axes, vmem_limit_bytes=vmem)


def _mod_kernel(c_ref, w_ref, b_ref, o_ref):
    c = c_ref[...]
    h = (c * jax.nn.sigmoid(c)).astype(BF16)
    o_ref[...] = jnp.dot(h, w_ref[...].astype(BF16), preferred_element_type=F32) + b_ref[...]


def _modulation(c_all, w, b, tn=1024):
    n_layers, d, n = w.shape
    r = c_all.shape[0]
    return pl.pallas_call(
        _mod_kernel,
        grid=(n_layers, n // tn),
        in_specs=[
            pl.BlockSpec((r, d), lambda l, j: (0, 0)),
            pl.BlockSpec((None, d, tn), lambda l, j: (l, 0, j)),
            pl.BlockSpec((None, 1, tn), lambda l, j: (l, 0, j)),
        ],
        out_specs=pl.BlockSpec((None, r, tn), lambda l, j: (l, 0, j)),
        out_shape=jax.ShapeDtypeStruct((n_layers, r, n), F32),
        compiler_params=_cparams(2),
        name="modulation",
    )(c_all, w, b.reshape(n_layers, 1, n))


class _Mod:
    def __init__(self, mod, n_sample, per_row, rows_per_seq=None):
        self.per_row = per_row
        self.n_sample = n_sample
        self.rows_per_seq = rows_per_seq
        n_layers, r, n = mod.shape
        self.arr = mod if per_row else mod.reshape(n_layers, r, 1, n)

    def spec(self, layer, chunk, width, tm, col_from_j):
        per = D_MODEL // width

        def col(rest):
            return chunk * per + (rest[0] if col_from_j else 0)

        if self.per_row:
            return pl.BlockSpec((None, self.n_sample, width), lambda i, *rest: (layer, 0, col(rest)))
        tiles_per_seq = self.rows_per_seq // tm
        base = self.n_sample
        return pl.BlockSpec((None, None, 1, width),
                            lambda i, *rest: (layer, base + i // tiles_per_seq, 0, col(rest)))


def _rope_tables(pos, head_dim):
    rot = head_dim // 4
    half = rot // 2
    inv = jnp.exp(jnp.arange(half, dtype=F32) * (-math.log(ROPE_THETA) / half))
    ang = pos.astype(F32)[:, None] * inv[None, :]
    cos, sin = jnp.cos(ang), jnp.sin(ang)
    lane = jnp.arange(LANES) % head_dim
    idx = lane % half
    c = jnp.where(lane[None, :] < rot, cos[:, idx], 1.0)
    s1 = jnp.where(((lane >= half) & (lane < rot))[None, :], sin[:, idx], 0.0)
    s2 = jnp.where((lane < half)[None, :], -sin[:, idx], 0.0)
    return c.astype(F32), s1.astype(F32), s2.astype(F32)


def _adaln(x, shift, scale):
    r = lax.rsqrt(jnp.mean(x * x, axis=-1, keepdims=True) + NORM_EPS)
    return x * r * (1.0 + scale) + shift


def _adaln_mm_kernel(x_ref, sh_ref, sc_ref, w_ref, g_ref, c_ref, s1_ref, s2_ref, o_ref, h_ref,
                     *, head_dim, n_norm, n_tiles, tn):
    j = pl.program_id(1)

    @pl.when(j == 0)
    def _():
        h_ref[...] = _adaln(x_ref[...], sh_ref[...], sc_ref[...]).astype(BF16)

    acc = jnp.dot(h_ref[...], w_ref[...].astype(BF16), preferred_element_type=F32)

    def normed():
        half = head_dim // 8
        gain = g_ref[...]
        c, s1, s2 = c_ref[...], s1_ref[...], s2_ref[...]
        for cb in range(tn // LANES):
            a = acc[:, cb * LANES:(cb + 1) * LANES]
            sq = a * a
            if head_dim == LANES:
                ms = jnp.mean(sq, axis=-1, keepdims=True)
            else:
                lo = lax.broadcasted_iota(jnp.int32, sq.shape, 1) < head_dim
                s_lo = jnp.sum(jnp.where(lo, sq, 0.0), axis=-1, keepdims=True)
                s_hi = jnp.sum(jnp.where(lo, 0.0, sq), axis=-1, keepdims=True)
                ms = jnp.where(lo, s_lo, s_hi) * (1.0 / head_dim)
            a = a * lax.rsqrt(ms + NORM_EPS) * gain
            a = a * c + pltpu.roll(a, half, 1) * s1 + pltpu.roll(a, LANES - half, 1) * s2
            o_ref[:, cb * LANES:(cb + 1) * LANES] = a

    if n_norm >= n_tiles:
        normed()
    elif n_norm == 0:
        o_ref[...] = acc
    else:
        pl.when(j < n_norm)(normed)

        @pl.when(j >= n_norm)
        def _():
            o_ref[...] = acc


def _adaln_matmul(x, mod, layer, chunks, w, w_layer, gains, tiles_per_gain, n_norm, head_dim, tables,
                  tm, tn, name):
    m, d = x.shape
    n = w.shape[-1]
    n_tiles = n // tn
    table_rows = tables[0].shape[0]
    table_tiles = table_rows // tm
    n_gains = gains.shape[0]
    tab_spec = pl.BlockSpec((tm, LANES), lambda i, j: (i % table_tiles, 0))
    kern = functools.partial(_adaln_mm_kernel, head_dim=head_dim, n_norm=n_norm, n_tiles=n_tiles, tn=tn)
    return pl.pallas_call(
        kern,
        grid=(m // tm, n_tiles),
        in_specs=[
            pl.BlockSpec((tm, d), lambda i, j: (i, 0)),
            mod.spec(layer, chunks[0], D_MODEL, tm, False),
            mod.spec(layer, chunks[1], D_MODEL, tm, False),
            pl.BlockSpec((None, d, tn), lambda i, j: (w_layer, 0, j)),
            pl.BlockSpec((None, 1, LANES), lambda i, j: (jnp.minimum(j // tiles_per_gain, n_gains - 1), 0, 0)),
            tab_spec, tab_spec, tab_spec,
        ],
        out_specs=pl.BlockSpec((tm, tn), lambda i, j: (i, j)),
        out_shape=jax.ShapeDtypeStruct((m, n), F32),
        scratch_shapes=[pltpu.VMEM((tm, d), BF16)],
        compiler_params=_cparams(2),
        name=name,
    )(x, mod.arr, mod.arr, w, gains, *tables)


def _oproj_kernel(o_ref, w_ref, x_ref, g_ref, out_ref):
    acc = jnp.dot(o_ref[...].astype(BF16), w_ref[...].astype(BF16), preferred_element_type=F32)
    out_ref[...] = x_ref[...] + g_ref[...] * acc


def _out_proj(o, w, w_layer, x, mod, layer, gate_chunk, tm, tn, name):
    m, d = x.shape
    k = o.shape[1]
    return pl.pallas_call(
        _oproj_kernel,
        grid=(m // tm, d // tn),
        in_specs=[
            pl.BlockSpec((tm, k), lambda i, j: (i, 0)),
            pl.BlockSpec((None, k, tn), lambda i, j: (w_layer, 0, j)),
            pl.BlockSpec((tm, tn), lambda i, j: (i, j)),
            mod.spec(layer, gate_chunk, tn, tm, True),
        ],
        out_specs=pl.BlockSpec((tm, tn), lambda i, j: (i, j)),
        out_shape=jax.ShapeDtypeStruct((m, d), F32),
        compiler_params=_cparams(2),
        name=name,
    )(o, w, x, mod.arr)


def _attn_a_prompt_kernel(q_ref, kp_ref, kc_ref, vp_ref, vc_ref, o_ref, kk, vv, ob, lb):
    first = pl.program_id(2) == 0
    qb = A_QBLOCK
    kk[0:qb, :] = kp_ref[...]
    kk[qb:2 * qb, :] = kc_ref[...]
    vv[0:qb, :] = vp_ref[...]
    vv[qb:2 * qb, :] = vc_ref[...]
    scale = A_HEAD_DIM ** -0.5
    qi = lax.broadcasted_iota(jnp.int32, (BLOCK, 2 * BLOCK), 0) + BLOCK
    kj = lax.broadcasted_iota(jnp.int32, (BLOCK, 2 * BLOCK), 1)
    dist = qi - kj
    band = (dist >= 0) & (dist <= BLOCK)
    band_first = band & (kj >= jnp.where(first, BLOCK, 0))
    for b, r in enumerate(A_BRANCH_DILATIONS):
        for rho in range(r):
            for m in range(qb // (BLOCK * r)):
                q0 = rho + BLOCK * r * m
                k0 = qb - BLOCK * r + q0
                if r == 1:
                    qsl, ksl = pl.ds(q0, BLOCK), pl.ds(k0, 2 * BLOCK)
                else:
                    qsl, ksl = pl.ds(q0, BLOCK, stride=r), pl.ds(k0, 2 * BLOCK, stride=r)
                q = q_ref[qsl, :].astype(BF16)
                k = kk[ksl, :].astype(BF16)
                v = vv[ksl, :].astype(BF16)
                s = lax.dot_general(q, k, (((1,), (1,)), ((), ())), preferred_element_type=F32) * scale
                mask = band_first if m == 0 else band
                s = jnp.where(mask, s, NEG_INF)
                mx = jnp.max(s, axis=-1, keepdims=True)
                p = jnp.exp(s - mx)
                den = jnp.sum(p, axis=-1, keepdims=True)
                o = jnp.dot(p.astype(BF16), v, preferred_element_type=F32) / den
                ob[b, qsl, :] = o
                lb[b, qsl, :] = jnp.broadcast_to(mx + jnp.log(den), (BLOCK, LANES))
    lse = [lb[b] for b in range(3)]
    top = jnp.maximum(jnp.maximum(lse[0], lse[1]), lse[2])
    w = [jnp.exp(l - top) for l in lse]
    tot = w[0] + w[1] + w[2]
    o_ref[...] = ((w[0] * ob[0] + w[1] * ob[1] + w[2] * ob[2]) / tot).astype(o_ref.dtype)


def _attn_a_prompt(qkv, n_seq, seq_len):
    h = A_HEADS
    qb = A_QBLOCK
    qkv3 = qkv.reshape(n_seq, seq_len, 3 * h * A_HEAD_DIM)

    def blk(col0, prev):
        if prev:
            return pl.BlockSpec((None, qb, LANES), lambda n, hh, t: (n, jnp.maximum(t - 1, 0), col0 + hh))
        return pl.BlockSpec((None, qb, LANES), lambda n, hh, t: (n, t, col0 + hh))

    out = pl.pallas_call(
        _attn_a_prompt_kernel,
        grid=(n_seq, h, seq_len // qb),
        in_specs=[blk(0, False), blk(h, True), blk(h, False), blk(2 * h, True), blk(2 * h, False)],
        out_specs=pl.BlockSpec((None, qb, LANES), lambda n, hh, t: (n, t, hh)),
        out_shape=jax.ShapeDtypeStruct((n_seq, seq_len, h * A_HEAD_DIM), BF16),
        scratch_shapes=[
            pltpu.VMEM((2 * qb, LANES), F32), pltpu.VMEM((2 * qb, LANES), F32),
            pltpu.VMEM((3, qb, LANES), F32), pltpu.VMEM((3, qb, LANES), F32),
        ],
        compiler_params=_cparams(3),
        name="attn_a_prompt",
    )(qkv3, qkv3, qkv3, qkv3, qkv3)
    return out.reshape(n_seq * seq_len, h * A_HEAD_DIM)


def _attn_b_prompt_kernel(sink_ref, q_ref, kvp_ref, kvc_ref, o_ref, kvs, *, tq):
    first = pl.program_id(1) == 0
    kv_w = 2 * B_KV_HEADS * B_HEAD_DIM
    k_cols = B_KV_HEADS * B_HEAD_DIM
    kvs[0:BLOCK, :] = kvp_ref[...]
    kvs[BLOCK:BLOCK + tq, :] = kvc_ref[...]
    scale = B_HEAD_DIM ** -0.5
    qi = lax.broadcasted_iota(jnp.int32, (BLOCK, 2 * BLOCK), 0) + BLOCK
    kj = lax.broadcasted_iota(jnp.int32, (BLOCK, 2 * BLOCK), 1)
    dist = qi - kj
    band = (dist >= 0) & (dist <= BLOCK)
    lane_half = lax.broadcasted_iota(jnp.int32, (2 * BLOCK, LANES), 1) // B_HEAD_DIM

    def sub_block(sb, carry):
        row0 = pl.multiple_of(sb * BLOCK, BLOCK)
        mask = band & (kj >= jnp.where(jnp.logical_and(first, sb == 0), BLOCK, 0))
        for hk in range(B_KV_HEADS):
            cbk, hh = hk // 2, hk % 2
            kblk = kvs[pl.ds(row0, 2 * BLOCK), cbk * LANES:(cbk + 1) * LANES]
            vblk = kvs[pl.ds(row0, 2 * BLOCK), k_cols + cbk * LANES:k_cols + (cbk + 1) * LANES]
            k_half, v_half = [], []
            for a in range(2):
                ka = kblk if a == hh else pltpu.roll(kblk, B_HEAD_DIM, 1)
                va = vblk if a == hh else pltpu.roll(vblk, B_HEAD_DIM, 1)
                k_half.append(jnp.where(lane_half == a, ka, 0.0).astype(BF16))
                v_half.append(jnp.where(lane_half == a, va, 0.0).astype(BF16))
            for c in range(hk * (B_GROUP // 2), (hk + 1) * (B_GROUP // 2)):
                q2 = q_ref[pl.ds(row0, BLOCK), c * LANES:(c + 1) * LANES].astype(BF16)
                o_pair = jnp.zeros((BLOCK, LANES), F32)
                for a in range(2):
                    sink = sink_ref[2 * c + a]
                    s = lax.dot_general(q2, k_half[a], (((1,), (1,)), ((), ())),
                                        preferred_element_type=F32) * scale
                    s = jnp.where(mask, s, NEG_INF)
                    mx = jnp.maximum(jnp.max(s, axis=-1, keepdims=True), sink)
                    p = jnp.exp(s - mx)
                    den = jnp.sum(p, axis=-1, keepdims=True) + jnp.exp(sink - mx)
                    o_pair = o_pair + jnp.dot(p.astype(BF16), v_half[a], preferred_element_type=F32) / den
                o_ref[pl.ds(row0, BLOCK), c * LANES:(c + 1) * LANES] = o_pair.astype(o_ref.dtype)
        return carry

    lax.fori_loop(0, tq // BLOCK, sub_block, 0)


def _attn_b_prompt(q, kv, sinks, n_seq, seq_len, tq=512):
    d = q.shape[1]
    kv_w = kv.shape[1]
    q3 = q.reshape(n_seq, seq_len, d)
    kv3 = kv.reshape(n_seq, seq_len, kv_w)
    per = tq // BLOCK
    out = pl.pallas_call(
        functools.partial(_attn_b_prompt_kernel, tq=tq),
        grid_spec=pltpu.PrefetchScalarGridSpec(
            num_scalar_prefetch=0,
            grid=(n_seq, seq_len // tq),
            in_specs=[
                pl.BlockSpec(memory_space=pltpu.SMEM),
                pl.BlockSpec((None, tq, d), lambda n, t: (n, t, 0)),
                pl.BlockSpec((None, BLOCK, kv_w), lambda n, t: (n, jnp.maximum(t * per - 1, 0), 0)),
                pl.BlockSpec((None, tq, kv_w), lambda n, t: (n, t, 0)),
            ],
            out_specs=pl.BlockSpec((None, tq, d), lambda n, t: (n, t, 0)),
            scratch_shapes=[pltpu.VMEM((BLOCK + tq, kv_w), F32)],
        ),
        out_shape=jax.ShapeDtypeStruct((n_seq, seq_len, d), BF16),
        compiler_params=_cparams(2),
        name="attn_b_prompt",
    )(sinks, q3, kv3, kv3)
    return out.reshape(n_seq * seq_len, d)


def _attn_a_step_kernel(q_ref, kn_ref, vn_ref, k1_ref, k4_ref, k16_ref, v1_ref, v4_ref, v16_ref, o_ref,
                        k2, v2):
    d = A_HEADS * A_HEAD_DIM
    scale = A_HEAD_DIM ** -0.5
    n_br = len(A_BRANCH_DILATIONS)
    for b, (k_ref, v_ref) in enumerate(((k1_ref, v1_ref), (k4_ref, v4_ref), (k16_ref, v16_ref))):
        for h in range(A_HEADS):
            k2[b, :, h * A_HEAD_DIM:(h + 1) * A_HEAD_DIM] = k_ref[:, h, :].astype(BF16)
            v2[b, :, h * A_HEAD_DIM:(h + 1) * A_HEAD_DIM] = v_ref[:, h, :].astype(BF16)
    q = q_ref[...]
    k_new = kn_ref[...]
    v_new = vn_ref[...]
    head = lax.broadcasted_iota(jnp.int32, (A_HEADS, d), 0)
    own = head == lax.broadcasted_iota(jnp.int32, (A_HEADS, d), 1) // A_HEAD_DIM
    qm = jnp.where(own, jnp.broadcast_to(q, (A_HEADS, d)), 0.0)
    s_new = jnp.sum(qm * k_new, axis=-1, keepdims=True) * scale
    qm16 = qm.astype(BF16)
    s = [lax.dot_general(qm16, k2[b], (((1,), (1,)), ((), ())), preferred_element_type=F32) * scale
         for b in range(n_br)]
    top = s_new
    for sb in s:
        top = jnp.maximum(top, jnp.max(sb, axis=-1, keepdims=True))
    p_new = jnp.exp(s_new - top)
    den = n_br * p_new
    acc = (n_br * p_new) * jnp.broadcast_to(v_new, (A_HEADS, d))
    for b, sb in enumerate(s):
        p = jnp.exp(sb - top)
        den = den + jnp.sum(p, axis=-1, keepdims=True)
        acc = acc + jnp.dot(p.astype(BF16), v2[b], preferred_element_type=F32)
    o = jnp.sum(jnp.where(own, acc / den, 0.0), axis=0, keepdims=True)
    o_ref[...] = o.astype(o_ref.dtype)


def _attn_a_step(qkv, cache_k, cache_v, layer):
    n = qkv.shape[0]
    d = A_HEADS * A_HEAD_DIM
    n_layers, _, buf = cache_k.shape[:3]
    assert buf == BLOCK * A_BRANCH_DILATIONS[-1]
    qkv3 = qkv.reshape(n, 1, 3 * d)

    def row(col):
        return pl.BlockSpec((None, 1, d), lambda i: (i, 0, col))

    views, specs = [], []
    for cache in (cache_k, cache_v):
        for r in A_BRANCH_DILATIONS:
            views.append(cache.reshape(n_layers, n, buf // r, r, A_HEADS, A_HEAD_DIM))
            last = buf // (r * BLOCK) - 1
            specs.append(pl.BlockSpec((None, None, BLOCK, None, A_HEADS, A_HEAD_DIM),
                                      lambda i, last=last: (layer, i, last, 0, 0, 0)))
    out = pl.pallas_call(
        _attn_a_step_kernel,
        grid=(n,),
        in_specs=[row(0), row(1), row(2)] + specs,
        out_specs=pl.BlockSpec((None, 1, d), lambda i: (i, 0, 0)),
        out_shape=jax.ShapeDtypeStruct((n, 1, d), BF16),
        scratch_shapes=[pltpu.VMEM((len(A_BRANCH_DILATIONS), BLOCK, d), BF16)] * 2,
        compiler_params=_cparams(1),
        name="attn_a_step",
    )(qkv3, qkv3, qkv3, *views)
    return out.reshape(n, d)


def _attn_b_step_kernel(q_ref, kvn_ref, ck_ref, cv_ref, sink_ref, o_ref):
    scale = B_HEAD_DIM ** -0.5
    for hk in range(B_KV_HEADS):
        rows = slice(hk * B_GROUP, (hk + 1) * B_GROUP)
        qg = q_ref[rows, :]
        k = ck_ref[:, hk, :]
        v = cv_ref[:, hk, :]
        k_new = kvn_ref[hk:hk + 1, :]
        v_new = kvn_ref[B_KV_HEADS + hk:B_KV_HEADS + hk + 1, :]
        sink = sink_ref[rows, :]
        s = lax.dot_general(qg.astype(BF16), k.astype(BF16), (((1,), (1,)), ((), ())),
                            preferred_element_type=F32) * scale
        s_new = jnp.sum(qg * k_new, axis=-1, keepdims=True) * scale
        top = jnp.maximum(jnp.maximum(jnp.max(s, axis=-1, keepdims=True), s_new), sink)
        p = jnp.exp(s - top)
        p_new = jnp.exp(s_new - top)
        den = jnp.sum(p, axis=-1, keepdims=True) + p_new + jnp.exp(sink - top)
        acc = jnp.dot(p.astype(BF16), v.astype(BF16), preferred_element_type=F32) + p_new * v_new
        o_ref[rows, :] = acc / den


def _attn_b_step(q, kv_new, cache_k, cache_v, sinks):
    n = q.shape[0]
    win = cache_k.shape[1]
    assert win == BLOCK
    q3 = q.reshape(n, B_HEADS, B_HEAD_DIM)
    kvn = kv_new.reshape(n, 2 * B_KV_HEADS, B_HEAD_DIM)
    cspec = pl.BlockSpec((None, win, B_KV_HEADS, B_HEAD_DIM), lambda i: (i, 0, 0, 0))
    out = pl.pallas_call(
        _attn_b_step_kernel,
        grid=(n,),
        in_specs=[
            pl.BlockSpec((None, B_HEADS, B_HEAD_DIM), lambda i: (i, 0, 0)),
            pl.BlockSpec((None, 2 * B_KV_HEADS, B_HEAD_DIM), lambda i: (i, 0, 0)),
            cspec, cspec,
            pl.BlockSpec((B_HEADS, 1), lambda i: (0, 0)),
        ],
        out_specs=pl.BlockSpec((None, B_HEADS, B_HEAD_DIM), lambda i: (i, 0, 0)),
        out_shape=jax.ShapeDtypeStruct((n, B_HEADS, B_HEAD_DIM), F32),
        compiler_params=_cparams(1),
        name="attn_b_step",
    )(q3, kvn, cache_k, cache_v, sinks.reshape(B_HEADS, 1))
    return out.reshape(n, B_HEADS * B_HEAD_DIM)


def _shift_b_kernel(kvn_ref, ck_ref, cv_ref, nk_ref, nv_ref):
    win = ck_ref.shape[0]
    nk_ref[0:win - 1] = ck_ref[1:win]
    nv_ref[0:win - 1] = cv_ref[1:win]
    nk_ref[win - 1] = kvn_ref[0:B_KV_HEADS, :]
    nv_ref[win - 1] = kvn_ref[B_KV_HEADS:2 * B_KV_HEADS, :]


def _shift_b_cache(kv_new, cache_k, cache_v):
    n, win = cache_k.shape[:2]
    kvn = kv_new.reshape(n, 2 * B_KV_HEADS, B_HEAD_DIM)
    cspec = pl.BlockSpec((None, win, B_KV_HEADS, B_HEAD_DIM), lambda i: (i, 0, 0, 0))
    return pl.pallas_call(
        _shift_b_kernel,
        grid=(n,),
        in_specs=[pl.BlockSpec((None, 2 * B_KV_HEADS, B_HEAD_DIM), lambda i: (i, 0, 0)), cspec, cspec],
        out_specs=[cspec, cspec],
        out_shape=[jax.ShapeDtypeStruct(cache_k.shape, F32), jax.ShapeDtypeStruct(cache_v.shape, F32)],
        compiler_params=_cparams(1),
        name="shift_b_cache",
    )(kvn, cache_k, cache_v)


def _shift_a_kernel(kn_ref, vn_ref, ck_ref, cv_ref, nk_ref, nv_ref, sem, *, n_layers, n_seq, buf):
    copies = []
    for l in range(n_layers):
        for new, src, dst in ((kn_ref, ck_ref, nk_ref), (vn_ref, cv_ref, nv_ref)):
            for n in range(n_seq):
                copies.append(pltpu.make_async_copy(
                    src.at[l, n, pl.ds(1, buf - 1)], dst.at[l, n, pl.ds(0, buf - 1)], sem.at[0]))
                copies.append(pltpu.make_async_copy(
                    new.at[l, n], dst.at[l, n, pl.ds(buf - 1, 1)], sem.at[1]))
    for c in copies:
        c.start()
    for c in copies:
        c.wait()


def _shift_a_cache(qkv_layers, cache_k, cache_v):
    n_layers, n_seq, buf = cache_k.shape[:3]
    d = A_HEADS * A_HEAD_DIM
    k_new = jnp.stack([q[:, d:2 * d] for q in qkv_layers]).reshape(n_layers, n_seq, 1, A_HEADS, A_HEAD_DIM)
    v_new = jnp.stack([q[:, 2 * d:] for q in qkv_layers]).reshape(n_layers, n_seq, 1, A_HEADS, A_HEAD_DIM)
    any_spec = pl.BlockSpec(memory_space=pl.ANY)
    return pl.pallas_call(
        functools.partial(_shift_a_kernel, n_layers=n_layers, n_seq=n_seq, buf=buf),
        in_specs=[any_spec] * 4,
        out_specs=[any_spec, any_spec],
        out_shape=[jax.ShapeDtypeStruct(cache_k.shape, F32), jax.ShapeDtypeStruct(cache_v.shape, F32)],
        scratch_shapes=[pltpu.SemaphoreType.DMA((2,))],
        name="shift_a_cache",
    )(k_new, v_new, cache_k, cache_v)


def _router_kernel(x_ref, sh_ref, sc_ref, rw_ref, rb_ref, h_ref, e_ref, w_ref):
    h = _adaln(x_ref[...], sh_ref[...], sc_ref[...])
    h_ref[...] = h
    logits = lax.dot_general(rw_ref[...], h, (((1,), (1,)), ((), ())),
                             precision=lax.Precision.HIGHEST, preferred_element_type=F32)
    scores = jax.nn.sigmoid(logits)
    sel = scores + rb_ref[...]
    tm = sel.shape[1]
    pos = lax.broadcasted_iota(jnp.int32, (EXPERTS_PER_GROUP, tm), 0)

    def top2(v):
        m1 = jnp.max(v, axis=0, keepdims=True)
        i1 = jnp.min(jnp.where(v == m1, pos, EXPERTS_PER_GROUP), axis=0, keepdims=True)
        v2 = jnp.where(pos == i1, -jnp.inf, v)
        m2 = jnp.max(v2, axis=0, keepdims=True)
        i2 = jnp.min(jnp.where(v2 == m2, pos, EXPERTS_PER_GROUP), axis=0, keepdims=True)
        return m1 + m2, i1, i2

    best, e1, e2 = None, None, None
    for g in range(N_GROUPS):
        gs, i1, i2 = top2(sel[g * EXPERTS_PER_GROUP:(g + 1) * EXPERTS_PER_GROUP, :])
        i1 = i1 + g * EXPERTS_PER_GROUP
        i2 = i2 + g * EXPERTS_PER_GROUP
        if g == 0:
            best, e1, e2 = gs, i1, i2
        else:
            take = gs > best
            best = jnp.where(take, gs, best)
            e1 = jnp.where(take, i1, e1)
            e2 = jnp.where(take, i2, e2)
    eid = lax.broadcasted_iota(jnp.int32, (N_EXPERTS, tm), 0)
    w1 = jnp.sum(jnp.where(eid == e1, scores, 0.0), axis=0, keepdims=True)
    w2 = jnp.sum(jnp.where(eid == e2, scores, 0.0), axis=0, keepdims=True)
    tot = w1 + w2
    e_ref[0:1, :] = e1
    e_ref[1:2, :] = e2
    w_ref[0:1, :] = w1 / tot
    w_ref[1:2, :] = w2 / tot


def _router(x, mod, layer, router_w_t, router_bias, tm, name):
    m, d = x.shape
    return pl.pallas_call(
        _router_kernel,
        grid=(m // tm,),
        in_specs=[
            pl.BlockSpec((tm, d), lambda i: (i, 0)),
            mod.spec(layer, 3, D_MODEL, tm, False),
            mod.spec(layer, 4, D_MODEL, tm, False),
            pl.BlockSpec((N_EXPERTS, d), lambda i: (0, 0)),
            pl.BlockSpec((N_EXPERTS, 1), lambda i: (0, 0)),
        ],
        out_specs=[
            pl.BlockSpec((tm, d), lambda i: (i, 0)),
            pl.BlockSpec((2, tm), lambda i: (0, i)),
            pl.BlockSpec((2, tm), lambda i: (0, i)),
        ],
        out_shape=[
            jax.ShapeDtypeStruct((m, d), F32),
            jax.ShapeDtypeStruct((2, m), jnp.int32),
            jax.ShapeDtypeStruct((2, m), F32),
        ],
        compiler_params=_cparams(1),
        name=name,
    )(x, mod.arr, mod.arr, router_w_t, router_bias.reshape(N_EXPERTS, 1))


DISPATCH_CHUNK = 1024


def _dispatch_kernel(slot_ref, hp_ref, hs_ref, xs_in_ref, xs_ref, sem, *, m_p, m_s):
    del xs_in_ref
    m_t = m_p + m_s

    def run(src_ref, count, slot_base):
        for c0 in range(0, count, DISPATCH_CHUNK):
            cn = min(DISPATCH_CHUNK, count - c0)

            def copy(t):
                return pltpu.make_async_copy(src_ref.at[pl.ds(c0 + t, 1)],
                                             xs_ref.at[pl.ds(slot_ref[slot_base + c0 + t], 1)], sem)

            def start(t, carry):
                copy(t).start()
                return carry

            def wait(t, carry):
                copy(t).wait()
                return carry

            lax.fori_loop(0, cn, start, 0)
            lax.fori_loop(0, cn, wait, 0)

    for k in range(2):
        run(hp_ref, m_p, k * m_t)
        run(hs_ref, m_s, k * m_t + m_p)


def _dispatch(slots, h_p, h_s, n_slots):
    m_p, d = h_p.shape
    m_s = h_s.shape[0]
    xs0 = jnp.zeros((n_slots, d), F32)
    any_spec = pl.BlockSpec(memory_space=pl.ANY)
    return pl.pallas_call(
        functools.partial(_dispatch_kernel, m_p=m_p, m_s=m_s),
        grid_spec=pltpu.PrefetchScalarGridSpec(
            num_scalar_prefetch=1,
            grid=(1,),
            in_specs=[any_spec, any_spec, any_spec],
            out_specs=any_spec,
            scratch_shapes=[pltpu.SemaphoreType.DMA(())],
        ),
        out_shape=jax.ShapeDtypeStruct((n_slots, d), F32),
        input_output_aliases={3: 0},
        compiler_params=_cparams(1),
        name="moe_dispatch",
    )(slots, h_p, h_s, xs0)


def _experts_kernel(te_ref, nu_ref, xs_ref, wg_ref, wu_ref, wd_ref, y_ref):
    t = pl.program_id(0)
    j = pl.program_id(1)

    @pl.when(j == 0)
    def _():
        y_ref[...] = jnp.zeros_like(y_ref)

    @pl.when(t < nu_ref[0])
    def _():
        x = xs_ref[...].astype(BF16)
        a = jnp.dot(x, wg_ref[...].astype(BF16), preferred_element_type=F32)
        u = jnp.dot(x, wu_ref[...].astype(BF16), preferred_element_type=F32)
        hidden = (a * jax.nn.sigmoid(a) * u).astype(BF16)
        y_ref[...] += jnp.dot(hidden, wd_ref[...].astype(BF16), preferred_element_type=F32)


def _experts(tile_expert, n_used, xs, w_gate, w_up, w_down, layer):
    n_slots, d = xs.shape
    f = w_gate.shape[-1]
    tm, tf = MOE_TM, MOE_TF
    nf = f // tf

    def col(t, j, nu):
        return jnp.where(t < nu[0], j, nf - 1)

    return pl.pallas_call(
        _experts_kernel,
        grid_spec=pltpu.PrefetchScalarGridSpec(
            num_scalar_prefetch=2,
            grid=(n_slots // tm, nf),
            in_specs=[
                pl.BlockSpec((tm, d), lambda t, j, te, nu: (jnp.minimum(t, nu[0] - 1), 0)),
                pl.BlockSpec((None, None, d, tf), lambda t, j, te, nu: (layer, te[t], 0, col(t, j, nu))),
                pl.BlockSpec((None, None, d, tf), lambda t, j, te, nu: (layer, te[t], 0, col(t, j, nu))),
                pl.BlockSpec((None, None, tf, d), lambda t, j, te, nu: (layer, te[t], col(t, j, nu), 0)),
            ],
            out_specs=pl.BlockSpec((tm, d), lambda t, j, te, nu: (t, 0)),
        ),
        out_shape=jax.ShapeDtypeStruct((n_slots, d), F32),
        compiler_params=_cparams(2),
        name="moe_experts",
    )(tile_expert, n_used, xs, w_gate, w_up, w_down)


def _combine_kernel(slot_ref, y_ref, x_ref, w_ref, g_ref, o_ref, buf, sem, *, tm, tok0, m_t):
    base = tok0 + pl.program_id(0) * tm

    def copy(t, k):
        return pltpu.make_async_copy(y_ref.at[pl.ds(slot_ref[k * m_t + base + t], 1)],
                                     buf.at[k, pl.ds(t, 1)], sem)

    def start(t, carry):
        copy(t, 0).start()
        copy(t, 1).start()
        return carry

    def wait(t, carry):
        copy(t, 0).wait()
        copy(t, 1).wait()
        return carry

    lax.fori_loop(0, tm, start, 0)
    lax.fori_loop(0, tm, wait, 0)
    w = w_ref[...]
    moe = w[:, 0:1] * buf[0] + w[:, 1:2] * buf[1]
    o_ref[...] = x_ref[...] + g_ref[...] * moe


def _combine(slots, y, x, w_t, mod, layer, tok0, m_t, tm, name):
    m, d = x.shape
    return pl.pallas_call(
        functools.partial(_combine_kernel, tm=tm, tok0=tok0, m_t=m_t),
        grid_spec=pltpu.PrefetchScalarGridSpec(
            num_scalar_prefetch=1,
            grid=(m // tm,),
            in_specs=[
                pl.BlockSpec(memory_space=pl.ANY),
                pl.BlockSpec((tm, d), lambda i, s: (i, 0)),
                pl.BlockSpec((tm, 2), lambda i, s: (tok0 // tm + i, 0)),
                mod.spec(layer, 5, D_MODEL, tm, False),
            ],
            out_specs=pl.BlockSpec((tm, d), lambda i, s: (i, 0)),
            scratch_shapes=[pltpu.VMEM((2, tm, d), F32), pltpu.SemaphoreType.DMA(())],
        ),
        out_shape=jax.ShapeDtypeStruct((m, d), F32),
        compiler_params=_cparams(1),
        name=name,
    )(slots, y, x, w_t, mod.arr)


def _slot_plan(e_idx, n_slots):
    tm = MOE_TM
    e_flat = e_idx.reshape(-1)
    onehot = (e_flat[:, None] == jnp.arange(N_EXPERTS, dtype=jnp.int32)[None, :]).astype(jnp.int32)
    csum = jnp.cumsum(onehot, axis=0)
    rank = jnp.take_along_axis(csum, e_flat[:, None], axis=1)[:, 0] - 1
    counts = csum[-1]
    padded = ((counts + tm - 1) // tm) * tm
    ends = jnp.cumsum(padded)
    slots = (ends - padded)[e_flat] + rank
    n_used = (ends[-1] // tm).astype(jnp.int32)
    tile_start = jnp.minimum(jnp.arange(n_slots // tm, dtype=jnp.int32), n_used - 1) * tm
    tile_expert = jnp.minimum(jnp.searchsorted(ends, tile_start, side="right"), N_EXPERTS - 1)
    return slots.astype(jnp.int32), tile_expert.astype(jnp.int32), n_used.reshape(1)


def _moe_layer(x_p, x_s, mod_p, mod_s, layer, router_w_t, router_bias, w_gate, w_up, w_down):
    m_p, d = x_p.shape
    m_s = x_s.shape[0]
    m_t = m_p + m_s
    tm = MOE_TM
    n_slots = -(-(2 * m_t + N_EXPERTS * (tm - 1)) // tm) * tm
    h_p, e_p, w_p = _router(x_p, mod_p, layer, router_w_t, router_bias, 512, "router_prompt")
    h_s, e_s, w_s = _router(x_s, mod_s, layer, router_w_t, router_bias, m_s, "router_sample")
    e_idx = jnp.concatenate([e_p, e_s], axis=1)
    w_t = jnp.concatenate([w_p, w_s], axis=1).T
    slots, tile_expert, n_used = _slot_plan(e_idx, n_slots)
    xs = _dispatch(slots, h_p, h_s, n_slots)
    y = _experts(tile_expert, n_used, xs, w_gate, w_up, w_down, layer)
    x_p = _combine(slots, y, x_p, w_t, mod_p, layer, 0, m_t, 256, "combine_prompt")
    x_s = _combine(slots, y, x_s, w_t, mod_s, layer, m_p, m_t, m_s, "combine_sample")
    return x_p, x_s


def kernel(x_prompt, x_sample, cache_a_k, cache_a_v, cache_b_k, cache_b_v, c_prompt, c_sample, a_w_qkv, a_q_gain, a_k_gain, a_w_o, b_w_q, b_q_gain, b_sinks, b_w_o, kv_w, kv_k_gain, kv_mod_w, kv_mod_b, mod_w, mod_b, router_w, router_bias, moe_w_gate, moe_w_up, moe_w_down):
    n_seq, seq_len, d = x_prompt.shape
    n_smp = x_sample.shape[0]
    m_p = n_seq * seq_len
    x_p = x_prompt.reshape(m_p, d)
    x_s = x_sample.reshape(n_smp, d)

    n_rows = -(-(n_smp + n_seq) // 8) * 8
    c_all = jnp.concatenate([c_sample, c_prompt, jnp.zeros((n_rows - n_smp - n_seq, d), F32)], axis=0)
    mod = _modulation(c_all, mod_w, mod_b)
    kv_mod = _modulation(c_all, kv_mod_w[None], kv_mod_b[None])
    mod_p = _Mod(mod, n_smp, False, seq_len)
    mod_s = _Mod(mod, n_smp, True)
    kvmod_p = _Mod(kv_mod, n_smp, False, seq_len)
    kvmod_s = _Mod(kv_mod, n_smp, True)

    pos_p = jnp.arange(seq_len, dtype=jnp.int32)
    pos_s = jnp.full((n_smp,), PAST_LEN, dtype=jnp.int32)
    tab_a_p, tab_a_s = _rope_tables(pos_p, A_HEAD_DIM), _rope_tables(pos_s, A_HEAD_DIM)
    tab_b_p, tab_b_s = _rope_tables(pos_p, B_HEAD_DIM), _rope_tables(pos_s, B_HEAD_DIM)

    router_w_t = router_w.T
    a_gains = jnp.stack([a_q_gain, a_k_gain], axis=1).reshape(N_A_LAYERS, 2, 1, A_HEAD_DIM)
    b_gains = jnp.concatenate([b_q_gain, b_q_gain], axis=-1).reshape(-1, 1, 1, LANES)
    kv_gain = jnp.concatenate([kv_k_gain, kv_k_gain]).reshape(1, 1, LANES)
    da = A_HEADS * A_HEAD_DIM

    tm_p, tn = 512, 512
    qkv_s_layers, ak_p, av_p = [], [], []
    kv_p = kv_s = None
    for layer in range(DEPTH):
        if layer < N_A_LAYERS:
            qkv_p = _adaln_matmul(x_p, mod_p, layer, (0, 1), a_w_qkv, layer, a_gains[layer], da // tn,
                                  2 * da // tn, A_HEAD_DIM, tab_a_p, tm_p, tn, "qkv_prompt")
            qkv_s = _adaln_matmul(x_s, mod_s, layer, (0, 1), a_w_qkv, layer, a_gains[layer], da // tn,
                                  2 * da // tn, A_HEAD_DIM, tab_a_s, n_smp, tn, "qkv_sample")
            o_p = _attn_a_prompt(qkv_p, n_seq, seq_len)
            o_s = _attn_a_step(qkv_s, cache_a_k, cache_a_v, layer)
            qkv_s_layers.append(qkv_s)
            keep = min(BLOCK * A_BRANCH_DILATIONS[-1], seq_len)
            qkv3 = qkv_p.reshape(n_seq, seq_len, 3 * da)
            ak_p.append(qkv3[:, seq_len - keep:, da:2 * da].reshape(n_seq, keep, A_HEADS, A_HEAD_DIM))
            av_p.append(qkv3[:, seq_len - keep:, 2 * da:].reshape(n_seq, keep, A_HEADS, A_HEAD_DIM))
            w_o, w_o_layer = a_w_o, layer
        else:
            jb = layer - N_A_LAYERS
            if layer == N_A_LAYERS:
                kvn = 2 * B_KV_HEADS * B_HEAD_DIM
                kv_p = _adaln_matmul(x_p, kvmod_p, 0, (0, 1), kv_w[None], 0, kv_gain, 1, 1, B_HEAD_DIM,
                                     tab_b_p, tm_p, kvn // 2, "kv_prompt")
                kv_s = _adaln_matmul(x_s, kvmod_s, 0, (0, 1), kv_w[None], 0, kv_gain, 1, 1, B_HEAD_DIM,
                                     tab_b_s, n_smp, kvn // 2, "kv_sample")
            q_p = _adaln_matmul(x_p, mod_p, layer, (0, 1), b_w_q, jb, b_gains[jb], d // tn, d // tn,
                                B_HEAD_DIM, tab_b_p, tm_p, tn, "q_prompt")
            q_s = _adaln_matmul(x_s, mod_s, layer, (0, 1), b_w_q, jb, b_gains[jb], d // tn, d // tn,
                                B_HEAD_DIM, tab_b_s, n_smp, tn, "q_sample")
            o_p = _attn_b_prompt(q_p, kv_p, b_sinks[jb], n_seq, seq_len)
            o_s = _attn_b_step(q_s, kv_s, cache_b_k, cache_b_v, b_sinks[jb])
            w_o, w_o_layer = b_w_o, jb
        x_p = _out_proj(o_p, w_o, w_o_layer, x_p, mod_p, layer, 2, tm_p, tn, "oproj_prompt")
        x_s = _out_proj(o_s, w_o, w_o_layer, x_s, mod_s, layer, 2, n_smp, tn, "oproj_sample")
        x_p, x_s = _moe_layer(x_p, x_s, mod_p, mod_s, layer, router_w_t, router_bias,
                              moe_w_gate, moe_w_up, moe_w_down)

    ak_s, av_s = _shift_a_cache(qkv_s_layers, cache_a_k, cache_a_v)
    bk_s, bv_s = _shift_b_cache(kv_s, cache_b_k, cache_b_v)
    keep_b = min(BLOCK, seq_len)
    kvh = B_KV_HEADS * B_HEAD_DIM
    kv_tail = kv_p.reshape(n_seq, seq_len, 2 * kvh)[:, seq_len - keep_b:]
    bk_p = kv_tail[..., :kvh].reshape(n_seq, keep_b, B_KV_HEADS, B_HEAD_DIM)
    bv_p = kv_tail[..., kvh:].reshape(n_seq, keep_b, B_KV_HEADS, B_HEAD_DIM)
    return (x_p.reshape(n_seq, seq_len, d), x_s.reshape(n_smp, 1, d),
            jnp.stack(ak_p), jnp.stack(av_p), bk_p, bv_p, ak_s, av_s, bk_s, bv_s)
```

```python
import functools
import math

import jax
import jax.numpy as jnp
from jax import lax
from jax.experimental import pallas as pl
from jax.experimental.pallas import tpu as pltpu

F32 = jnp.float32
BF16 = jnp.bfloat16

D_MODEL = 2048
DEPTH = 4
N_A_LAYERS = DEPTH // 2
PAST_LEN = 16384
A_HEADS = 16
A_HEAD_DIM = 128
A_BRANCH_DILATIONS = (1, 4, 16)
B_HEADS = 32
B_HEAD_DIM = 64
B_KV_HEADS = 4
B_GROUP = B_HEADS // B_KV_HEADS
ROPE_THETA = 500000.0
N_EXPERTS = 16
N_GROUPS = 4
EXPERTS_PER_GROUP = N_EXPERTS // N_GROUPS
D_EXPERT = D_MODEL // 2
BLOCK = 128
LANES = 128
NORM_EPS = 1e-6
NEG_INF = -1e30

A_QBLOCK = 2048
MOE_SUB = 256
MOE_CHUNK_SUBS = 5
MOE_TF = 256
VMEM_LIMIT = 56 * 1024 * 1024


def _cparams(n_axes, vmem=VMEM_LIMIT):
    return pltpu.CompilerParams(dimension_semantics=("arbitrary",) * n_axes, vmem_limit_bytes=vmem)


def _mod_kernel(c_ref, w_ref, b_ref, o_ref):
    c = c_ref[...]
    h = (c * jax.nn.sigmoid(c)).astype(BF16)
    o_ref[...] = jnp.dot(h, w_ref[...].astype(BF16), preferred_element_type=F32) + b_ref[...]


def _modulation(c_all, w, b, tn=1024):
    n_layers, d, n = w.shape
    r = c_all.shape[0]
    return pl.pallas_call(
        _mod_kernel,
        grid=(n_layers, n // tn),
        in_specs=[
            pl.BlockSpec((r, d), lambda l, j: (0, 0)),
            pl.BlockSpec((None, d, tn), lambda l, j: (l, 0, j)),
            pl.BlockSpec((None, 1, tn), lambda l, j: (l, 0, j)),
        ],
        out_specs=pl.BlockSpec((None, r, tn), lambda l, j: (l, 0, j)),
        out_shape=jax.ShapeDtypeStruct((n_layers, r, n), F32),
        compiler_params=_cparams(2),
        name="modulation",
    )(c_all, w, b.reshape(n_layers, 1, n))


class _Mod:
    def __init__(self, mod, n_sample, per_row, rows_per_seq=None):
        self.per_row = per_row
        self.n_sample = n_sample
        self.rows_per_seq = rows_per_seq
        n_layers, r, n = mod.shape
        self.arr = mod if per_row else mod.reshape(n_layers, r, 1, n)

    def spec(self, layer, chunk, width, tm, col_from_j):
        per = D_MODEL // width

        def col(rest):
            return chunk * per + (rest[0] if col_from_j else 0)

        if self.per_row:
            return pl.BlockSpec((None, self.n_sample, width), lambda i, *rest: (layer, 0, col(rest)))
        tiles_per_seq = self.rows_per_seq // tm
        base = self.n_sample
        return pl.BlockSpec((None, None, 1, width),
                            lambda i, *rest: (layer, base + i // tiles_per_seq, 0, col(rest)))


def _rope_tables(pos, head_dim):
    rot = head_dim // 4
    half = rot // 2
    inv = jnp.exp(jnp.arange(half, dtype=F32) * (-math.log(ROPE_THETA) / half))
    ang = pos.astype(F32)[:, None] * inv[None, :]
    cos, sin = jnp.cos(ang), jnp.sin(ang)
    lane = jnp.arange(LANES) % head_dim
    idx = lane % half
    c = jnp.where(lane[None, :] < rot, cos[:, idx], 1.0)
    s1 = jnp.where(((lane >= half) & (lane < rot))[None, :], sin[:, idx], 0.0)
    s2 = jnp.where((lane < half)[None, :], -sin[:, idx], 0.0)
    return c.astype(F32), s1.astype(F32), s2.astype(F32)


def _adaln(x, shift, scale):
    r = lax.rsqrt(jnp.mean(x * x, axis=-1, keepdims=True) + NORM_EPS)
    return x * r * (1.0 + scale) + shift


def _adaln_mm_kernel(x_ref, sh_ref, sc_ref, w_ref, g_ref, c_ref, s1_ref, s2_ref, o_ref, h_ref,
                     *, head_dim, n_norm, n_tiles, tn):
    j = pl.program_id(1)

    @pl.when(j == 0)
    def _():
        h_ref[...] = _adaln(x_ref[...], sh_ref[...], sc_ref[...]).astype(BF16)

    acc = jnp.dot(h_ref[...], w_ref[...].astype(BF16), preferred_element_type=F32)

    def normed():
        half = head_dim // 8
        gain = g_ref[...]
        c, s1, s2 = c_ref[...], s1_ref[...], s2_ref[...]
        for cb in range(tn // LANES):
            a = acc[:, cb * LANES:(cb + 1) * LANES]
            sq = a * a
            if head_dim == LANES:
                ms = jnp.mean(sq, axis=-1, keepdims=True)
            else:
                lo = lax.broadcasted_iota(jnp.int32, sq.shape, 1) < head_dim
                s_lo = jnp.sum(jnp.where(lo, sq, 0.0), axis=-1, keepdims=True)
                s_hi = jnp.sum(jnp.where(lo, 0.0, sq), axis=-1, keepdims=True)
                ms = jnp.where(lo, s_lo, s_hi) * (1.0 / head_dim)
            a = a * lax.rsqrt(ms + NORM_EPS) * gain
            a = a * c + pltpu.roll(a, half, 1) * s1 + pltpu.roll(a, LANES - half, 1) * s2
            o_ref[:, cb * LANES:(cb + 1) * LANES] = a

    if n_norm >= n_tiles:
        normed()
    elif n_norm == 0:
        o_ref[...] = acc
    else:
        pl.when(j < n_norm)(normed)

        @pl.when(j >= n_norm)
        def _():
            o_ref[...] = acc


def _adaln_matmul(x, mod, layer, chunks, w, w_layer, gains, tiles_per_gain, n_norm, head_dim, tables,
                  tm, tn, name):
    m, d = x.shape
    n = w.shape[-1]
    n_tiles = n // tn
    table_rows = tables[0].shape[0]
    table_tiles = table_rows // tm
    n_gains = gains.shape[0]
    tab_spec = pl.BlockSpec((tm, LANES), lambda i, j: (i % table_tiles, 0))
    kern = functools.partial(_adaln_mm_kernel, head_dim=head_dim, n_norm=n_norm, n_tiles=n_tiles, tn=tn)
    return pl.pallas_call(
        kern,
        grid=(m // tm, n_tiles),
        in_specs=[
            pl.BlockSpec((tm, d), lambda i, j: (i, 0)),
            mod.spec(layer, chunks[0], D_MODEL, tm, False),
            mod.spec(layer, chunks[1], D_MODEL, tm, False),
            pl.BlockSpec((None, d, tn), lambda i, j: (w_layer, 0, j)),
            pl.BlockSpec((None, 1, LANES), lambda i, j: (jnp.minimum(j // tiles_per_gain, n_gains - 1), 0, 0)),
            tab_spec, tab_spec, tab_spec,
        ],
        out_specs=pl.BlockSpec((tm, tn), lambda i, j: (i, j)),
        out_shape=jax.ShapeDtypeStruct((m, n), F32),
        scratch_shapes=[pltpu.VMEM((tm, d), BF16)],
        compiler_params=_cparams(2),
        name=name,
    )(x, mod.arr, mod.arr, w, gains, *tables)


def _oproj_kernel(o_ref, w_ref, x_ref, g_ref, out_ref):
    acc = jnp.dot(o_ref[...].astype(BF16), w_ref[...].astype(BF16), preferred_element_type=F32)
    out_ref[...] = x_ref[...] + g_ref[...] * acc


def _out_proj(o, w, w_layer, x, mod, layer, gate_chunk, tm, tn, name):
    m, d = x.shape
    k = o.shape[1]
    return pl.pallas_call(
        _oproj_kernel,
        grid=(m // tm, d // tn),
        in_specs=[
            pl.BlockSpec((tm, k), lambda i, j: (i, 0)),
            pl.BlockSpec((None, k, tn), lambda i, j: (w_layer, 0, j)),
            pl.BlockSpec((tm, tn), lambda i, j: (i, j)),
            mod.spec(layer, gate_chunk, tn, tm, True),
        ],
        out_specs=pl.BlockSpec((tm, tn), lambda i, j: (i, j)),
        out_shape=jax.ShapeDtypeStruct((m, d), F32),
        compiler_params=_cparams(2),
        name=name,
    )(o, w, x, mod.arr)


def _attn_a_prompt_kernel(q_ref, kp_ref, kc_ref, vp_ref, vc_ref, o_ref, kk, vv, ob, lb):
    first = pl.program_id(2) == 0
    qb = A_QBLOCK
    kk[0:qb, :] = kp_ref[...]
    kk[qb:2 * qb, :] = kc_ref[...]
    vv[0:qb, :] = vp_ref[...]
    vv[qb:2 * qb, :] = vc_ref[...]
    scale = A_HEAD_DIM ** -0.5
    qi = lax.broadcasted_iota(jnp.int32, (BLOCK, 2 * BLOCK), 0) + BLOCK
    kj = lax.broadcasted_iota(jnp.int32, (BLOCK, 2 * BLOCK), 1)
    dist = qi - kj
    band = (dist >= 0) & (dist <= BLOCK)
    band_first = band & (kj >= jnp.where(first, BLOCK, 0))
    for b, r in enumerate(A_BRANCH_DILATIONS):
        for rho in range(r):
            for m in range(qb // (BLOCK * r)):
                q0 = rho + BLOCK * r * m
                k0 = qb - BLOCK * r + q0
                if r == 1:
                    qsl, ksl = pl.ds(q0, BLOCK), pl.ds(k0, 2 * BLOCK)
                else:
                    qsl, ksl = pl.ds(q0, BLOCK, stride=r), pl.ds(k0, 2 * BLOCK, stride=r)
                q = q_ref[qsl, :].astype(BF16)
                k = kk[ksl, :].astype(BF16)
                v = vv[ksl, :].astype(BF16)
                s = lax.dot_general(q, k, (((1,), (1,)), ((), ())), preferred_element_type=F32) * scale
                mask = band_first if m == 0 else band
                s = jnp.where(mask, s, NEG_INF)
                mx = jnp.max(s, axis=-1, keepdims=True)
                p = jnp.exp(s - mx)
                den = jnp.sum(p, axis=-1, keepdims=True)
                o = jnp.dot(p.astype(BF16), v, preferred_element_type=F32) / den
                ob[b, qsl, :] = o
                lb[b, qsl, :] = jnp.broadcast_to(mx + jnp.log(den), (BLOCK, LANES))
    lse = [lb[b] for b in range(3)]
    top = jnp.maximum(jnp.maximum(lse[0], lse[1]), lse[2])
    w = [jnp.exp(l - top) for l in lse]
    tot = w[0] + w[1] + w[2]
    o_ref[...] = ((w[0] * ob[0] + w[1] * ob[1] + w[2] * ob[2]) / tot).astype(o_ref.dtype)


def _attn_a_prompt(qkv, n_seq, seq_len):
    h = A_HEADS
    qb = A_QBLOCK
    qkv3 = qkv.reshape(n_seq, seq_len, 3 * h * A_HEAD_DIM)

    def blk(col0, prev):
        if prev:
            return pl.BlockSpec((None, qb, LANES), lambda n, hh, t: (n, jnp.maximum(t - 1, 0), col0 + hh))
        return pl.BlockSpec((None, qb, LANES), lambda n, hh, t: (n, t, col0 + hh))

    out = pl.pallas_call(
        _attn_a_prompt_kernel,
        grid=(n_seq, h, seq_len // qb),
        in_specs=[blk(0, False), blk(h, True), blk(h, False), blk(2 * h, True), blk(2 * h, False)],
        out_specs=pl.BlockSpec((None, qb, LANES), lambda n, hh, t: (n, t, hh)),
        out_shape=jax.ShapeDtypeStruct((n_seq, seq_len, h * A_HEAD_DIM), BF16),
        scratch_shapes=[
            pltpu.VMEM((2 * qb, LANES), F32), pltpu.VMEM((2 * qb, LANES), F32),
            pltpu.VMEM((3, qb, LANES), F32), pltpu.VMEM((3, qb, LANES), F32),
        ],
        compiler_params=_cparams(3),
        name="attn_a_prompt",
    )(qkv3, qkv3, qkv3, qkv3, qkv3)
    return out.reshape(n_seq * seq_len, h * A_HEAD_DIM)


def _attn_b_prompt_kernel(sink_ref, q_ref, kvp_ref, kvc_ref, o_ref, kvs, *, tq):
    first = pl.program_id(1) == 0
    kv_w = 2 * B_KV_HEADS * B_HEAD_DIM
    k_cols = B_KV_HEADS * B_HEAD_DIM
    kvs[0:BLOCK, :] = kvp_ref[...]
    kvs[BLOCK:BLOCK + tq, :] = kvc_ref[...]
    scale = B_HEAD_DIM ** -0.5
    qi = lax.broadcasted_iota(jnp.int32, (BLOCK, 2 * BLOCK), 0) + BLOCK
    kj = lax.broadcasted_iota(jnp.int32, (BLOCK, 2 * BLOCK), 1)
    dist = qi - kj
    band = (dist >= 0) & (dist <= BLOCK)
    lane_half = lax.broadcasted_iota(jnp.int32, (2 * BLOCK, LANES), 1) // B_HEAD_DIM

    def sub_block(sb, carry):
        row0 = pl.multiple_of(sb * BLOCK, BLOCK)
        mask = band & (kj >= jnp.where(jnp.logical_and(first, sb == 0), BLOCK, 0))
        for hk in range(B_KV_HEADS):
            cbk, hh = hk // 2, hk % 2
            kblk = kvs[pl.ds(row0, 2 * BLOCK), cbk * LANES:(cbk + 1) * LANES]
            vblk = kvs[pl.ds(row0, 2 * BLOCK), k_cols + cbk * LANES:k_cols + (cbk + 1) * LANES]
            k_half, v_half = [], []
            for a in range(2):
                ka = kblk if a == hh else pltpu.roll(kblk, B_HEAD_DIM, 1)
                va = vblk if a == hh else pltpu.roll(vblk, B_HEAD_DIM, 1)
                k_half.append(jnp.where(lane_half == a, ka, 0.0).astype(BF16))
                v_half.append(jnp.where(lane_half == a, va, 0.0).astype(BF16))
            for c in range(hk * (B_GROUP // 2), (hk + 1) * (B_GROUP // 2)):
                q2 = q_ref[pl.ds(row0, BLOCK), c * LANES:(c + 1) * LANES].astype(BF16)
                o_pair = jnp.zeros((BLOCK, LANES), F32)
                for a in range(2):
                    sink = sink_ref[2 * c + a]
                    s = lax.dot_general(q2, k_half[a], (((1,), (1,)), ((), ())),
                                        preferred_element_type=F32) * scale
                    s = jnp.where(mask, s, NEG_INF)
                    mx = jnp.maximum(jnp.max(s, axis=-1, keepdims=True), sink)
                    p = jnp.exp(s - mx)
                    den = jnp.sum(p, axis=-1, keepdims=True) + jnp.exp(sink - mx)
                    o_pair = o_pair + jnp.dot(p.astype(BF16), v_half[a], preferred_element_type=F32) / den
                o_ref[pl.ds(row0, BLOCK), c * LANES:(c + 1) * LANES] = o_pair.astype(o_ref.dtype)
        return carry

    lax.fori_loop(0, tq // BLOCK, sub_block, 0)


def _attn_b_prompt(q, kv, sinks, n_seq, seq_len, tq=512):
    d = q.shape[1]
    kv_w = kv.shape[1]
    q3 = q.reshape(n_seq, seq_len, d)
    kv3 = kv.reshape(n_seq, seq_len, kv_w)
    per = tq // BLOCK
    out = pl.pallas_call(
        functools.partial(_attn_b_prompt_kernel, tq=tq),
        grid_spec=pltpu.PrefetchScalarGridSpec(
            num_scalar_prefetch=0,
            grid=(n_seq, seq_len // tq),
            in_specs=[
                pl.BlockSpec(memory_space=pltpu.SMEM),
                pl.BlockSpec((None, tq, d), lambda n, t: (n, t, 0)),
                pl.BlockSpec((None, BLOCK, kv_w), lambda n, t: (n, jnp.maximum(t * per - 1, 0), 0)),
                pl.BlockSpec((None, tq, kv_w), lambda n, t: (n, t, 0)),
            ],
            out_specs=pl.BlockSpec((None, tq, d), lambda n, t: (n, t, 0)),
            scratch_shapes=[pltpu.VMEM((BLOCK + tq, kv_w), F32)],
        ),
        out_shape=jax.ShapeDtypeStruct((n_seq, seq_len, d), BF16),
        compiler_params=_cparams(2),
        name="attn_b_prompt",
    )(sinks, q3, kv3, kv3)
    return out.reshape(n_seq * seq_len, d)


def _attn_a_step_kernel(qkv_ref, k1_ref, k4_ref, k16_ref, v1_ref, v4_ref, v16_ref, o_ref):
    scale = A_HEAD_DIM ** -0.5
    n_br = len(A_BRANCH_DILATIONS)
    q = qkv_ref[0:A_HEADS, :]
    k_new = qkv_ref[A_HEADS:2 * A_HEADS, :]
    v_new = qkv_ref[2 * A_HEADS:3 * A_HEADS, :]
    s_new = jnp.sum(q * k_new, axis=-1, keepdims=True) * scale
    s = [jnp.sum(k_ref[...] * q[None], axis=-1, keepdims=True) * scale
         for k_ref in (k1_ref, k4_ref, k16_ref)]
    top = s_new
    for sb in s:
        top = jnp.maximum(top, jnp.max(sb, axis=0))
    p_new = jnp.exp(s_new - top)
    den = n_br * p_new
    acc = (n_br * p_new) * v_new
    for sb, v_ref in zip(s, (v1_ref, v4_ref, v16_ref)):
        p = jnp.exp(sb - top[None])
        den = den + jnp.sum(p, axis=0)
        acc = acc + jnp.sum(p * v_ref[...], axis=0)
    o_ref[...] = (acc / den).astype(o_ref.dtype)


def _attn_a_step(qkv, cache_k, cache_v, layer):
    n = qkv.shape[0]
    d = A_HEADS * A_HEAD_DIM
    n_layers, _, buf = cache_k.shape[:3]
    assert buf == BLOCK * A_BRANCH_DILATIONS[-1]
    qkv3 = qkv.reshape(n, 3 * A_HEADS, A_HEAD_DIM)

    views, specs = [], []
    for cache in (cache_k, cache_v):
        for r in A_BRANCH_DILATIONS:
            views.append(cache.reshape(n_layers, n, buf // r, r, A_HEADS, A_HEAD_DIM))
            last = buf // (r * BLOCK) - 1
            specs.append(pl.BlockSpec((None, None, BLOCK, None, A_HEADS, A_HEAD_DIM),
                                      lambda i, last=last: (layer, i, last, 0, 0, 0)))
    out = pl.pallas_call(
        _attn_a_step_kernel,
        grid=(n,),
        in_specs=[pl.BlockSpec((None, 3 * A_HEADS, A_HEAD_DIM), lambda i: (i, 0, 0))] + specs,
        out_specs=pl.BlockSpec((None, A_HEADS, A_HEAD_DIM), lambda i: (i, 0, 0)),
        out_shape=jax.ShapeDtypeStruct((n, A_HEADS, A_HEAD_DIM), BF16),
        compiler_params=_cparams(1),
        name="attn_a_step",
    )(qkv3, *views)
    return out.reshape(n, d)


def _attn_b_step_kernel(q_ref, kvn_ref, ck_ref, cv_ref, sink_ref, o_ref):
    scale = B_HEAD_DIM ** -0.5
    for hk in range(B_KV_HEADS):
        rows = slice(hk * B_GROUP, (hk + 1) * B_GROUP)
        qg = q_ref[rows, :]
        k = ck_ref[:, hk, :]
        v = cv_ref[:, hk, :]
        k_new = kvn_ref[hk:hk + 1, :]
        v_new = kvn_ref[B_KV_HEADS + hk:B_KV_HEADS + hk + 1, :]
        sink = sink_ref[rows, :]
        s = lax.dot_general(qg.astype(BF16), k.astype(BF16), (((1,), (1,)), ((), ())),
                            preferred_element_type=F32) * scale
        s_new = jnp.sum(qg * k_new, axis=-1, keepdims=True) * scale
        top = jnp.maximum(jnp.maximum(jnp.max(s, axis=-1, keepdims=True), s_new), sink)
        p = jnp.exp(s - top)
        p_new = jnp.exp(s_new - top)
        den = jnp.sum(p, axis=-1, keepdims=True) + p_new + jnp.exp(sink - top)
        acc = jnp.dot(p.astype(BF16), v.astype(BF16), preferred_element_type=F32) + p_new * v_new
        o_ref[rows, :] = acc / den


def _attn_b_step(q, kv_new, cache_k, cache_v, sinks):
    n = q.shape[0]
    win = cache_k.shape[1]
    assert win == BLOCK
    q3 = q.reshape(n, B_HEADS, B_HEAD_DIM)
    kvn = kv_new.reshape(n, 2 * B_KV_HEADS, B_HEAD_DIM)
    cspec = pl.BlockSpec((None, win, B_KV_HEADS, B_HEAD_DIM), lambda i: (i, 0, 0, 0))
    out = pl.pallas_call(
        _attn_b_step_kernel,
        grid=(n,),
        in_specs=[
            pl.BlockSpec((None, B_HEADS, B_HEAD_DIM), lambda i: (i, 0, 0)),
            pl.BlockSpec((None, 2 * B_KV_HEADS, B_HEAD_DIM), lambda i: (i, 0, 0)),
            cspec, cspec,
            pl.BlockSpec((B_HEADS, 1), lambda i: (0, 0)),
        ],
        out_specs=pl.BlockSpec((None, B_HEADS, B_HEAD_DIM), lambda i: (i, 0, 0)),
        out_shape=jax.ShapeDtypeStruct((n, B_HEADS, B_HEAD_DIM), F32),
        compiler_params=_cparams(1),
        name="attn_b_step",
    )(q3, kvn, cache_k, cache_v, sinks.reshape(B_HEADS, 1))
    return out.reshape(n, B_HEADS * B_HEAD_DIM)


def _shift_b_kernel(kvn_ref, ck_ref, cv_ref, nk_ref, nv_ref):
    win = ck_ref.shape[0]
    nk_ref[0:win - 1] = ck_ref[1:win]
    nv_ref[0:win - 1] = cv_ref[1:win]
    nk_ref[win - 1] = kvn_ref[0:B_KV_HEADS, :]
    nv_ref[win - 1] = kvn_ref[B_KV_HEADS:2 * B_KV_HEADS, :]


def _shift_b_cache(kv_new, cache_k, cache_v):
    n, win = cache_k.shape[:2]
    kvn = kv_new.reshape(n, 2 * B_KV_HEADS, B_HEAD_DIM)
    cspec = pl.BlockSpec((None, win, B_KV_HEADS, B_HEAD_DIM), lambda i: (i, 0, 0, 0))
    return pl.pallas_call(
        _shift_b_kernel,
        grid=(n,),
        in_specs=[pl.BlockSpec((None, 2 * B_KV_HEADS, B_HEAD_DIM), lambda i: (i, 0, 0)), cspec, cspec],
        out_specs=[cspec, cspec],
        out_shape=[jax.ShapeDtypeStruct(cache_k.shape, F32), jax.ShapeDtypeStruct(cache_v.shape, F32)],
        compiler_params=_cparams(1),
        name="shift_b_cache",
    )(kvn, cache_k, cache_v)


SHIFT_ROWS = 256


def _shift_a_kernel(kc_ref, kx_ref, kn_ref, vc_ref, vx_ref, vn_ref, nk_ref, nv_ref):
    t = pl.program_id(2)
    last = pl.num_programs(2) - 1
    rows = nk_ref.shape[0]
    for cur, nxt, new, out in ((kc_ref, kx_ref, kn_ref, nk_ref), (vc_ref, vx_ref, vn_ref, nv_ref)):
        out[0:rows - 1] = cur[1:rows]

        @pl.when(t < last)
        def _(nxt=nxt, out=out):
            out[rows - 1] = nxt[0]

        @pl.when(t == last)
        def _(new=new, out=out):
            out[rows - 1] = new[0]


def _shift_a_cache(qkv_layers, cache_k, cache_v):
    n_layers, n_seq, buf = cache_k.shape[:3]
    d = A_HEADS * A_HEAD_DIM
    rows = SHIFT_ROWS
    k_new = jnp.stack([q[:, d:2 * d] for q in qkv_layers]).reshape(n_layers, n_seq, 1, A_HEADS, A_HEAD_DIM)
    v_new = jnp.stack([q[:, 2 * d:] for q in qkv_layers]).reshape(n_layers, n_seq, 1, A_HEADS, A_HEAD_DIM)
    cur = pl.BlockSpec((None, None, rows, A_HEADS, A_HEAD_DIM), lambda l, n, t: (l, n, t, 0, 0))
    nxt = pl.BlockSpec((None, None, 1, A_HEADS, A_HEAD_DIM),
                       lambda l, n, t: (l, n, jnp.minimum((t + 1) * rows, buf - 1), 0, 0))
    new = pl.BlockSpec((None, None, 1, A_HEADS, A_HEAD_DIM), lambda l, n, t: (l, n, 0, 0, 0))
    return pl.pallas_call(
        _shift_a_kernel,
        grid=(n_layers, n_seq, buf // rows),
        in_specs=[cur, nxt, new, cur, nxt, new],
        out_specs=[cur, cur],
        out_shape=[jax.ShapeDtypeStruct(cache_k.shape, F32), jax.ShapeDtypeStruct(cache_v.shape, F32)],
        compiler_params=_cparams(3),
        name="shift_a_cache",
    )(cache_k, cache_k, k_new, cache_v, cache_v, v_new)


def _router_kernel(x_ref, sh_ref, sc_ref, rw_ref, rb_ref, h_ref, e_ref, w_ref):
    h = _adaln(x_ref[...], sh_ref[...], sc_ref[...])
    h_ref[...] = h
    logits = lax.dot_general(rw_ref[...], h, (((1,), (1,)), ((), ())),
                             precision=lax.Precision.HIGHEST, preferred_element_type=F32)
    scores = jax.nn.sigmoid(logits)
    sel = scores + rb_ref[...]
    tm = sel.shape[1]
    pos = lax.broadcasted_iota(jnp.int32, (EXPERTS_PER_GROUP, tm), 0)

    def top2(v):
        m1 = jnp.max(v, axis=0, keepdims=True)
        i1 = jnp.min(jnp.where(v == m1, pos, EXPERTS_PER_GROUP), axis=0, keepdims=True)
        v2 = jnp.where(pos == i1, -jnp.inf, v)
        m2 = jnp.max(v2, axis=0, keepdims=True)
        i2 = jnp.min(jnp.where(v2 == m2, pos, EXPERTS_PER_GROUP), axis=0, keepdims=True)
        return m1 + m2, i1, i2

    best, e1, e2 = None, None, None
    for g in range(N_GROUPS):
        gs, i1, i2 = top2(sel[g * EXPERTS_PER_GROUP:(g + 1) * EXPERTS_PER_GROUP, :])
        i1 = i1 + g * EXPERTS_PER_GROUP
        i2 = i2 + g * EXPERTS_PER_GROUP
        if g == 0:
            best, e1, e2 = gs, i1, i2
        else:
            take = gs > best
            best = jnp.where(take, gs, best)
            e1 = jnp.where(take, i1, e1)
            e2 = jnp.where(take, i2, e2)
    eid = lax.broadcasted_iota(jnp.int32, (N_EXPERTS, tm), 0)
    w1 = jnp.sum(jnp.where(eid == e1, scores, 0.0), axis=0, keepdims=True)
    w2 = jnp.sum(jnp.where(eid == e2, scores, 0.0), axis=0, keepdims=True)
    tot = w1 + w2
    e_ref[0:1, :] = e1
    e_ref[1:2, :] = e2
    w_ref[0:1, :] = w1 / tot
    w_ref[1:2, :] = w2 / tot


def _router_into_kernel(x_ref, sh_ref, sc_ref, rw_ref, rb_ref, h_all_ref, h_ref, e_ref, w_ref):
    del h_all_ref
    _router_kernel(x_ref, sh_ref, sc_ref, rw_ref, rb_ref, h_ref, e_ref, w_ref)


def _router_alloc_kernel(x_ref, sh_ref, sc_ref, rw_ref, rb_ref, h_ref, e_ref, w_ref, *, n_real):
    i = pl.program_id(0)

    @pl.when(i < n_real)
    def _():
        _router_kernel(x_ref, sh_ref, sc_ref, rw_ref, rb_ref, h_ref, e_ref, w_ref)

    @pl.when(i >= n_real)
    def _():
        h_ref[...] = jnp.zeros_like(h_ref)


def _router(x, mod, layer, router_w_t, router_bias, tm, m_total, row0, h_all, name):
    m, d = x.shape
    blk0 = row0 // tm
    n_real = m // tm
    n_steps = n_real if h_all is not None else -(-m_total // tm)
    real = lambda i: jnp.minimum(i, n_real - 1)
    in_specs = [
        pl.BlockSpec((tm, d), lambda i: (real(i), 0)),
        mod.spec(layer, 3, D_MODEL, tm, False),
        mod.spec(layer, 4, D_MODEL, tm, False),
        pl.BlockSpec((N_EXPERTS, d), lambda i: (0, 0)),
        pl.BlockSpec((N_EXPERTS, 1), lambda i: (0, 0)),
    ]
    args = [x, mod.arr, mod.arr, router_w_t, router_bias.reshape(N_EXPERTS, 1)]
    if h_all is not None:
        in_specs.append(pl.BlockSpec(memory_space=pl.ANY))
        args.append(h_all)
        kern = _router_into_kernel
    else:
        kern = functools.partial(_router_alloc_kernel, n_real=n_real)
    return pl.pallas_call(
        kern,
        grid=(n_steps,),
        in_specs=in_specs,
        out_specs=[
            pl.BlockSpec((tm, d), lambda i: (blk0 + i, 0)),
            pl.BlockSpec((2, tm), lambda i: (0, real(i))),
            pl.BlockSpec((2, tm), lambda i: (0, real(i))),
        ],
        out_shape=[
            jax.ShapeDtypeStruct((m_total, d), F32),
            jax.ShapeDtypeStruct((2, m), jnp.int32),
            jax.ShapeDtypeStruct((2, m), F32),
        ],
        input_output_aliases={} if h_all is None else {len(args) - 1: 0},
        compiler_params=_cparams(1),
        name=name,
    )(*args)


def _row_gather(idx_of_row, src_ref, dst_ref, sem, n_rows):
    def copy(r):
        return pltpu.make_async_copy(src_ref.at[pl.ds(idx_of_row(r), 1)], dst_ref.at[pl.ds(r, 1)], sem)

    def start():
        def body(r, carry):
            copy(r).start()
            return carry
        lax.fori_loop(0, n_rows, body, 0, unroll=8)

    def wait():
        def body(r, carry):
            copy(r).wait()
            return carry
        lax.fori_loop(0, n_rows, body, 0, unroll=8)

    return start, wait


def _gather_kernel(src_ref, vt_ref, nv_ref, h_ref, xs0_ref, xs_ref, buf, sem):
    del xs0_ref
    i = pl.program_id(0)
    nv = nv_ref[0]
    cur = i % 2

    def tile(t, b):
        base = vt_ref[t] * MOE_SUB
        return _row_gather(lambda r: src_ref[base + r], h_ref, buf.at[b], sem.at[b], MOE_SUB)

    @pl.when(i == 0)
    def _():
        tile(0, 0)[0]()

    @pl.when(i < nv)
    def _():
        tile(i, cur)[1]()

        @pl.when(i + 1 < nv)
        def _():
            tile(i + 1, 1 - cur)[0]()

        xs_ref[...] = buf[cur].astype(BF16)


def _dispatch(src_tok, live_tiles, n_live, h_all, n_slots, max_live):
    d = h_all.shape[1]
    return pl.pallas_call(
        _gather_kernel,
        grid_spec=pltpu.PrefetchScalarGridSpec(
            num_scalar_prefetch=3,
            grid=(max_live,),
            in_specs=[pl.BlockSpec(memory_space=pl.ANY), pl.BlockSpec(memory_space=pl.ANY)],
            out_specs=pl.BlockSpec((MOE_SUB, d), lambda i, src, vt, nv: (vt[jnp.minimum(i, nv[0] - 1)], 0)),
            scratch_shapes=[pltpu.VMEM((2, MOE_SUB, d), F32), pltpu.SemaphoreType.DMA((2,))],
        ),
        out_shape=jax.ShapeDtypeStruct((n_slots, d), BF16),
        input_output_aliases={4: 0},
        compiler_params=_cparams(1),
        name="moe_dispatch",
    )(src_tok, live_tiles, n_live, h_all, jnp.zeros((n_slots, d), BF16))


def _experts_kernel(ce_ref, cs_ref, nu_ref, xs_ref, wg_ref, wu_ref, wd_ref, y_ref):
    c = pl.program_id(0)
    j = pl.program_id(1)

    @pl.when(c < nu_ref[0])
    def _():
        wg = wg_ref[...].astype(BF16)
        wu = wu_ref[...].astype(BF16)
        wd = wd_ref[...].astype(BF16)
        n_live = cs_ref[c]
        for k in range(MOE_CHUNK_SUBS):
            rows = slice(k * MOE_SUB, (k + 1) * MOE_SUB)

            @pl.when(k < n_live)
            def _(rows=rows):
                x = xs_ref[rows, :]
                a = jnp.dot(x, wg, preferred_element_type=F32)
                u = jnp.dot(x, wu, preferred_element_type=F32)
                hidden = (a * jax.nn.sigmoid(a) * u).astype(BF16)
                part = jnp.dot(hidden, wd, preferred_element_type=F32)

                @pl.when(j == 0)
                def _():
                    y_ref[rows, :] = part

                @pl.when(j > 0)
                def _():
                    y_ref[rows, :] += part

            @pl.when(jnp.logical_and(k >= n_live, j == 0))
            def _(rows=rows):
                y_ref[rows, :] = jnp.zeros((MOE_SUB, y_ref.shape[1]), F32)

    @pl.when(jnp.logical_and(c >= nu_ref[0], j == 0))
    def _():
        y_ref[...] = jnp.zeros_like(y_ref)


def _experts(chunk_expert, chunk_subs, n_used, xs, w_gate, w_up, w_down, layer):
    n_slots, d = xs.shape
    f = w_gate.shape[-1]
    tf = MOE_TF
    nf = f // tf
    ch = MOE_SUB * MOE_CHUNK_SUBS

    def row(c, nu):
        return jnp.minimum(c, nu[0] - 1)

    def col(c, j, nu):
        return jnp.where(c < nu[0], j, nf - 1)

    return pl.pallas_call(
        _experts_kernel,
        grid_spec=pltpu.PrefetchScalarGridSpec(
            num_scalar_prefetch=3,
            grid=(n_slots // ch, nf),
            in_specs=[
                pl.BlockSpec((ch, d), lambda c, j, ce, cs, nu: (row(c, nu), 0)),
                pl.BlockSpec((None, None, d, tf), lambda c, j, ce, cs, nu: (layer, ce[c], 0, col(c, j, nu))),
                pl.BlockSpec((None, None, d, tf), lambda c, j, ce, cs, nu: (layer, ce[c], 0, col(c, j, nu))),
                pl.BlockSpec((None, None, tf, d), lambda c, j, ce, cs, nu: (layer, ce[c], col(c, j, nu), 0)),
            ],
            out_specs=pl.BlockSpec((ch, d), lambda c, j, ce, cs, nu: (c, 0)),
        ),
        out_shape=jax.ShapeDtypeStruct((n_slots, d), F32),
        compiler_params=_cparams(2),
        name="moe_experts",
    )(chunk_expert, chunk_subs, n_used, xs, w_gate, w_up, w_down)


def _combine_kernel(slot_ref, y_ref, x_ref, w_ref, g_ref, o_ref, buf, sem, *, tm, tok0, m_t):
    i = pl.program_id(0)
    n = pl.num_programs(0)
    cur = i % 2

    def tile(t, b, k):
        base = k * m_t + tok0 + t * tm
        return _row_gather(lambda r: slot_ref[base + r], y_ref, buf.at[b, k], sem.at[b], tm)

    @pl.when(i == 0)
    def _():
        tile(0, 0, 0)[0]()
        tile(0, 0, 1)[0]()

    tile(i, cur, 0)[1]()
    tile(i, cur, 1)[1]()

    @pl.when(i + 1 < n)
    def _():
        tile(i + 1, 1 - cur, 0)[0]()
        tile(i + 1, 1 - cur, 1)[0]()

    w = w_ref[...]
    moe = w[:, 0:1] * buf[cur, 0] + w[:, 1:2] * buf[cur, 1]
    o_ref[...] = x_ref[...] + g_ref[...] * moe


def _combine(slots, y, x, w_t, mod, layer, tok0, m_t, tm, name):
    m, d = x.shape
    return pl.pallas_call(
        functools.partial(_combine_kernel, tm=tm, tok0=tok0, m_t=m_t),
        grid_spec=pltpu.PrefetchScalarGridSpec(
            num_scalar_prefetch=1,
            grid=(m // tm,),
            in_specs=[
                pl.BlockSpec(memory_space=pl.ANY),
                pl.BlockSpec((tm, d), lambda i, s: (i, 0)),
                pl.BlockSpec((tm, 2), lambda i, s: (tok0 // tm + i, 0)),
                mod.spec(layer, 5, D_MODEL, tm, False),
            ],
            out_specs=pl.BlockSpec((tm, d), lambda i, s: (i, 0)),
            scratch_shapes=[pltpu.VMEM((2, 2, tm, d), F32), pltpu.SemaphoreType.DMA((2,))],
        ),
        out_shape=jax.ShapeDtypeStruct((m, d), F32),
        compiler_params=_cparams(1),
        name=name,
    )(slots, y, x, w_t, mod.arr)


def _slot_plan(e_idx, max_chunks, max_live):
    sub, per = MOE_SUB, MOE_CHUNK_SUBS
    m_t = e_idx.shape[1]
    e_flat = e_idx.reshape(-1)
    onehot = (e_flat[:, None] == jnp.arange(N_EXPERTS, dtype=jnp.int32)[None, :]).astype(jnp.int32)
    csum = jnp.cumsum(onehot, axis=0)
    rank = jnp.take_along_axis(csum, e_flat[:, None], axis=1)[:, 0] - 1
    counts = csum[-1]
    n_sub = (counts + sub - 1) // sub
    n_chunk = (n_sub + per - 1) // per
    chunk_end = jnp.cumsum(n_chunk)
    chunk_base = chunk_end - n_chunk
    slots = (chunk_base * (per * sub))[e_flat] + rank
    tok = jnp.tile(jnp.arange(m_t, dtype=jnp.int32), 2)
    src_tok = jnp.zeros((max_chunks * per * sub,), jnp.int32).at[slots].set(tok, unique_indices=True)
    n_used = chunk_end[-1]
    c = jnp.minimum(jnp.arange(max_chunks, dtype=jnp.int32), n_used - 1)
    chunk_expert = jnp.minimum(jnp.searchsorted(chunk_end, c, side="right"), N_EXPERTS - 1)
    chunk_subs = jnp.clip(n_sub[chunk_expert] - (c - chunk_base[chunk_expert]) * per, 0, per)
    sub_end = jnp.cumsum(n_sub)
    n_live = sub_end[-1]
    t = jnp.minimum(jnp.arange(max_live, dtype=jnp.int32), n_live - 1)
    t_expert = jnp.minimum(jnp.searchsorted(sub_end, t, side="right"), N_EXPERTS - 1)
    live_tiles = chunk_base[t_expert] * per + (t - (sub_end - n_sub)[t_expert])
    i32 = lambda a: a.astype(jnp.int32)
    return (i32(slots), src_tok, i32(live_tiles), i32(n_live).reshape(1), i32(chunk_expert), i32(chunk_subs),
            i32(n_used).reshape(1))


def _moe_layer(x_p, x_s, mod_p, mod_s, layer, router_w_t, router_bias, w_gate, w_up, w_down):
    m_p, d = x_p.shape
    m_s = x_s.shape[0]
    m_t = m_p + m_s
    sub, per = MOE_SUB, MOE_CHUNK_SUBS
    max_live = (2 * m_t) // sub + N_EXPERTS
    max_chunks = max_live // per + N_EXPERTS
    n_slots = max_chunks * per * sub
    h_all, e_p, w_p = _router(x_p, mod_p, layer, router_w_t, router_bias, 512, m_t, 0, None, "router_prompt")
    h_all, e_s, w_s = _router(x_s, mod_s, layer, router_w_t, router_bias, m_s, m_t, m_p, h_all, "router_sample")
    e_idx = jnp.concatenate([e_p, e_s], axis=1)
    w_t = jnp.concatenate([w_p, w_s], axis=1).T
    slots, src_tok, live_tiles, n_live, chunk_expert, chunk_subs, n_used = _slot_plan(e_idx, max_chunks, max_live)
    xs = _dispatch(src_tok, live_tiles, n_live, h_all, n_slots, max_live)
    y = _experts(chunk_expert, chunk_subs, n_used, xs, w_gate, w_up, w_down, layer)
    x_p = _combine(slots, y, x_p, w_t, mod_p, layer, 0, m_t, 256, "combine_prompt")
    x_s = _combine(slots, y, x_s, w_t, mod_s, layer, m_p, m_t, m_s, "combine_sample")
    return x_p, x_s


def kernel(x_prompt, x_sample, cache_a_k, cache_a_v, cache_b_k, cache_b_v, c_prompt, c_sample, a_w_qkv, a_q_gain, a_k_gain, a_w_o, b_w_q, b_q_gain, b_sinks, b_w_o, kv_w, kv_k_gain, kv_mod_w, kv_mod_b, mod_w, mod_b, router_w, router_bias, moe_w_gate, moe_w_up, moe_w_down):
    n_seq, seq_len, d = x_prompt.shape
    n_smp = x_sample.shape[0]
    m_p = n_seq * seq_len
    x_p = x_prompt.reshape(m_p, d)
    x_s = x_sample.reshape(n_smp, d)

    n_rows = -(-(n_smp + n_seq) // 8) * 8
    c_all = jnp.concatenate([c_sample, c_prompt, jnp.zeros((n_rows - n_smp - n_seq, d), F32)], axis=0)
    mod = _modulation(c_all, mod_w, mod_b)
    kv_mod = _modulation(c_all, kv_mod_w[None], kv_mod_b[None])
    mod_p = _Mod(mod, n_smp, False, seq_len)
    mod_s = _Mod(mod, n_smp, True)
    kvmod_p = _Mod(kv_mod, n_smp, False, seq_len)
    kvmod_s = _Mod(kv_mod, n_smp, True)

    pos_p = jnp.arange(seq_len, dtype=jnp.int32)
    pos_s = jnp.full((n_smp,), PAST_LEN, dtype=jnp.int32)
    tab_a_p, tab_a_s = _rope_tables(pos_p, A_HEAD_DIM), _rope_tables(pos_s, A_HEAD_DIM)
    tab_b_p, tab_b_s = _rope_tables(pos_p, B_HEAD_DIM), _rope_tables(pos_s, B_HEAD_DIM)

    router_w_t = router_w.T
    a_gains = jnp.stack([a_q_gain, a_k_gain], axis=1).reshape(N_A_LAYERS, 2, 1, A_HEAD_DIM)
    b_gains = jnp.concatenate([b_q_gain, b_q_gain], axis=-1).reshape(-1, 1, 1, LANES)
    kv_gain = jnp.concatenate([kv_k_gain, kv_k_gain]).reshape(1, 1, LANES)
    da = A_HEADS * A_HEAD_DIM

    tm_p, tn = 1024, 512
    qkv_s_layers, ak_p, av_p = [], [], []
    kv_p = kv_s = None
    for layer in range(DEPTH):
        if layer < N_A_LAYERS:
            qkv_p = _adaln_matmul(x_p, mod_p, layer, (0, 1), a_w_qkv, layer, a_gains[layer], da // tn,
                                  2 * da // tn, A_HEAD_DIM, tab_a_p, tm_p, tn, "qkv_prompt")
            qkv_s = _adaln_matmul(x_s, mod_s, layer, (0, 1), a_w_qkv, layer, a_gains[layer], da // tn,
                                  2 * da // tn, A_HEAD_DIM, tab_a_s, n_smp, tn, "qkv_sample")
            o_p = _attn_a_prompt(qkv_p, n_seq, seq_len)
            o_s = _attn_a_step(qkv_s, cache_a_k, cache_a_v, layer)
            qkv_s_layers.append(qkv_s)
            keep = min(BLOCK * A_BRANCH_DILATIONS[-1], seq_len)
            qkv3 = qkv_p.reshape(n_seq, seq_len, 3 * da)
            ak_p.append(qkv3[:, seq_len - keep:, da:2 * da].reshape(n_seq, keep, A_HEADS, A_HEAD_DIM))
            av_p.append(qkv3[:, seq_len - keep:, 2 * da:].reshape(n_seq, keep, A_HEADS, A_HEAD_DIM))
            w_o, w_o_layer = a_w_o, layer
        else:
            jb = layer - N_A_LAYERS
            if layer == N_A_LAYERS:
                kvn = 2 * B_KV_HEADS * B_HEAD_DIM
                kv_p = _adaln_matmul(x_p, kvmod_p, 0, (0, 1), kv_w[None], 0, kv_gain, 1, 1, B_HEAD_DIM,
                                     tab_b_p, tm_p, kvn // 2, "kv_prompt")
                kv_s = _adaln_matmul(x_s, kvmod_s, 0, (0, 1), kv_w[None], 0, kv_gain, 1, 1, B_HEAD_DIM,
                                     tab_b_s, n_smp, kvn // 2, "kv_sample")
            q_p = _adaln_matmul(x_p, mod_p, layer, (0, 1), b_w_q, jb, b_gains[jb], d // tn, d // tn,
                                B_HEAD_DIM, tab_b_p, tm_p, tn, "q_prompt")
            q_s = _adaln_matmul(x_s, mod_s, layer, (0, 1), b_w_q, jb, b_gains[jb], d // tn, d // tn,
                                B_HEAD_DIM, tab_b_s, n_smp, tn, "q_sample")
            o_p = _attn_b_prompt(q_p, kv_p, b_sinks[jb], n_seq, seq_len)
            o_s = _attn_b_step(q_s, kv_s, cache_b_k, cache_b_v, b_sinks[jb])
            w_o, w_o_layer = b_w_o, jb
        x_p = _out_proj(o_p, w_o, w_o_layer, x_p, mod_p, layer, 2, tm_p, tn, "oproj_prompt")
        x_s = _out_proj(o_s, w_o, w_o_layer, x_s, mod_s, layer, 2, n_smp, tn, "oproj_sample")
        x_p, x_s = _moe_layer(x_p, x_s, mod_p, mod_s, layer, router_w_t, router_bias,
                              moe_w_gate, moe_w_up, moe_w_down)

    ak_s, av_s = _shift_a_cache(qkv_s_layers, cache_a_k, cache_a_v)
    bk_s, bv_s = _shift_b_cache(kv_s, cache_b_k, cache_b_v)
    keep_b = min(BLOCK, seq_len)
    kvh = B_KV_HEADS * B_HEAD_DIM
    kv_tail = kv_p.reshape(n_seq, seq_len, 2 * kvh)[:, seq_len - keep_b:]
    bk_p = kv_tail[..., :kvh].reshape(n_seq, keep_b, B_KV_HEADS, B_HEAD_DIM)
    bv_p = kv_tail[..., kvh:].reshape(n_seq, keep_b, B_KV_HEADS, B_HEAD_DIM)
    return (x_p.reshape(n_seq, seq_len, d), x_s.reshape(n_smp, 1, d),
            jnp.stack(ak_p), jnp.stack(av_p), bk_p, bv_p, ak_s, av_s, bk_s, bv_s)
```

```python
import functools
import math

import jax
import jax.numpy as jnp
from jax import lax
from jax.experimental import pallas as pl
from jax.experimental.pallas import tpu as pltpu

F32 = jnp.float32
BF16 = jnp.bfloat16

D_MODEL = 2048
DEPTH = 4
N_A_LAYERS = DEPTH // 2
PAST_LEN = 16384
A_HEADS = 16
A_HEAD_DIM = 128
A_BRANCH_DILATIONS = (1, 4, 16)
B_HEADS = 32
B_HEAD_DIM = 64
B_KV_HEADS = 4
B_GROUP = B_HEADS // B_KV_HEADS
ROPE_THETA = 500000.0
N_EXPERTS = 16
N_GROUPS = 4
EXPERTS_PER_GROUP = N_EXPERTS // N_GROUPS
D_EXPERT = D_MODEL // 2
BLOCK = 128
LANES = 128
NORM_ROWS = 256
NORM_EPS = 1e-6
NEG_INF = -1e30

A_QBLOCK = 2048
MOE_SUB = 256
MOE_CHUNK_SUBS = 5
MOE_TF = 256
MOE_DOWN_COLS = 512
VMEM_LIMIT = 56 * 1024 * 1024


def _cparams(n_axes, vmem=VMEM_LIMIT):
    return pltpu.CompilerParams(dimension_semantics=("arbitrary",) * n_axes, vmem_limit_bytes=vmem)


def _mod_kernel(c_ref, w_ref, b_ref, o_ref):
    c = c_ref[...]
    h = (c * jax.nn.sigmoid(c)).astype(BF16)
    o_ref[...] = jnp.dot(h, w_ref[...].astype(BF16), preferred_element_type=F32) + b_ref[...]


def _modulation(c_all, w, b, tn=1024):
    n_layers, d, n = w.shape
    r = c_all.shape[0]
    return pl.pallas_call(
        _mod_kernel,
        grid=(n_layers, n // tn),
        in_specs=[
            pl.BlockSpec((r, d), lambda l, j: (0, 0)),
            pl.BlockSpec((None, d, tn), lambda l, j: (l, 0, j)),
            pl.BlockSpec((None, 1, tn), lambda l, j: (l, 0, j)),
        ],
        out_specs=pl.BlockSpec((None, r, tn), lambda l, j: (l, 0, j)),
        out_shape=jax.ShapeDtypeStruct((n_layers, r, n), F32),
        compiler_params=_cparams(2),
        name="modulation",
    )(c_all, w, b.reshape(n_layers, 1, n))


class _Mod:
    def __init__(self, mod, n_sample, per_row, rows_per_seq=None):
        self.per_row = per_row
        self.n_sample = n_sample
        self.rows_per_seq = rows_per_seq
        n_layers, r, n = mod.shape
        self.arr = mod if per_row else mod.reshape(n_layers, r, 1, n)

    def spec(self, layer, chunk, width, tm, col_from_j):
        per = D_MODEL // width

        def col(rest):
            return chunk * per + (rest[0] if col_from_j else 0)

        if self.per_row:
            return pl.BlockSpec((None, self.n_sample, width), lambda i, *rest: (layer, 0, col(rest)))
        tiles_per_seq = self.rows_per_seq // tm
        base = self.n_sample
        return pl.BlockSpec((None, None, 1, width),
                            lambda i, *rest: (layer, base + i // tiles_per_seq, 0, col(rest)))


def _rope_tables(pos, head_dim):
    rot = head_dim // 4
    half = rot // 2
    inv = jnp.exp(jnp.arange(half, dtype=F32) * (-math.log(ROPE_THETA) / half))
    ang = pos.astype(F32)[:, None] * inv[None, :]
    cos, sin = jnp.cos(ang), jnp.sin(ang)
    lane = jnp.arange(LANES) % head_dim
    idx = lane % half
    c = jnp.where(lane[None, :] < rot, cos[:, idx], 1.0)
    s1 = jnp.where(((lane >= half) & (lane < rot))[None, :], sin[:, idx], 0.0)
    s2 = jnp.where((lane < half)[None, :], -sin[:, idx], 0.0)
    return c.astype(F32), s1.astype(F32), s2.astype(F32)


def _adaln(x, shift, scale):
    r = lax.rsqrt(jnp.mean(x * x, axis=-1, keepdims=True) + NORM_EPS)
    return x * r * (1.0 + scale) + shift


def _adaln_mm_kernel(x_ref, sh_ref, sc_ref, w_ref, g_ref, c_ref, s1_ref, s2_ref, o_ref, h_ref,
                     *, head_dim, n_norm, n_tiles, tn, row_groups):
    j = pl.program_id(1)

    @pl.when(j == 0)
    def _():
        h_ref[...] = _adaln(x_ref[...], sh_ref[...], sc_ref[...]).astype(BF16)

    def plain():
        o_ref[...] = jnp.dot(h_ref[...], w_ref[...].astype(BF16), preferred_element_type=F32)

    def normed():
        w16 = w_ref[...].astype(BF16)
        rows_per = h_ref.shape[0] // row_groups
        for rg in range(row_groups):
            rows = slice(rg * rows_per, (rg + 1) * rows_per)
            acc = jnp.dot(h_ref[rows, :], w16, preferred_element_type=F32)
            normed_rows(acc, rows)

    def normed_rows(acc, rows):
        half = head_dim // 8
        gain = g_ref[...]
        c, s1, s2 = c_ref[rows, :], s1_ref[rows, :], s2_ref[rows, :]
        for cb in range(tn // LANES):
            a = acc[:, cb * LANES:(cb + 1) * LANES]
            sq = a * a
            if head_dim == LANES:
                ms = jnp.mean(sq, axis=-1, keepdims=True)
            else:
                lo = lax.broadcasted_iota(jnp.int32, sq.shape, 1) < head_dim
                s_lo = jnp.sum(jnp.where(lo, sq, 0.0), axis=-1, keepdims=True)
                s_hi = jnp.sum(jnp.where(lo, 0.0, sq), axis=-1, keepdims=True)
                ms = jnp.where(lo, s_lo, s_hi) * (1.0 / head_dim)
            a = a * lax.rsqrt(ms + NORM_EPS) * gain
            a = a * c + pltpu.roll(a, half, 1) * s1 + pltpu.roll(a, LANES - half, 1) * s2
            o_ref[rows, cb * LANES:(cb + 1) * LANES] = a

    if n_norm >= n_tiles:
        normed()
    elif n_norm == 0:
        plain()
    else:
        pl.when(j < n_norm)(normed)
        pl.when(j >= n_norm)(plain)


def _adaln_matmul(x, mod, layer, chunks, w, w_layer, gains, tiles_per_gain, n_norm, head_dim, tables,
                  tm, tn, name):
    m, d = x.shape
    n = w.shape[-1]
    n_tiles = n // tn
    table_rows = tables[0].shape[0]
    table_tiles = table_rows // tm
    n_gains = gains.shape[0]
    tab_spec = pl.BlockSpec((tm, LANES), lambda i, j: (i % table_tiles, 0))
    kern = functools.partial(_adaln_mm_kernel, head_dim=head_dim, n_norm=n_norm, n_tiles=n_tiles, tn=tn,
                             row_groups=max(1, tm // NORM_ROWS))
    return pl.pallas_call(
        kern,
        grid=(m // tm, n_tiles),
        in_specs=[
            pl.BlockSpec((tm, d), lambda i, j: (i, 0)),
            mod.spec(layer, chunks[0], D_MODEL, tm, False),
            mod.spec(layer, chunks[1], D_MODEL, tm, False),
            pl.BlockSpec((None, d, tn), lambda i, j: (w_layer, 0, j)),
            pl.BlockSpec((None, 1, LANES), lambda i, j: (jnp.minimum(j // tiles_per_gain, n_gains - 1), 0, 0)),
            tab_spec, tab_spec, tab_spec,
        ],
        out_specs=pl.BlockSpec((tm, tn), lambda i, j: (i, j)),
        out_shape=jax.ShapeDtypeStruct((m, n), F32),
        scratch_shapes=[pltpu.VMEM((tm, d), BF16)],
        compiler_params=_cparams(2),
        name=name,
    )(x, mod.arr, mod.arr, w, gains, *tables)


def _oproj_kernel(o_ref, w_ref, x_ref, g_ref, out_ref):
    acc = jnp.dot(o_ref[...].astype(BF16), w_ref[...].astype(BF16), preferred_element_type=F32)
    out_ref[...] = x_ref[...] + g_ref[...] * acc


def _out_proj(o, w, w_layer, x, mod, layer, gate_chunk, tm, tn, name):
    m, d = x.shape
    k = o.shape[1]
    return pl.pallas_call(
        _oproj_kernel,
        grid=(m // tm, d // tn),
        in_specs=[
            pl.BlockSpec((tm, k), lambda i, j: (i, 0)),
            pl.BlockSpec((None, k, tn), lambda i, j: (w_layer, 0, j)),
            pl.BlockSpec((tm, tn), lambda i, j: (i, j)),
            mod.spec(layer, gate_chunk, tn, tm, True),
        ],
        out_specs=pl.BlockSpec((tm, tn), lambda i, j: (i, j)),
        out_shape=jax.ShapeDtypeStruct((m, d), F32),
        compiler_params=_cparams(2),
        name=name,
    )(o, w, x, mod.arr)


def _attn_a_prompt_kernel(q_ref, kp_ref, kc_ref, vp_ref, vc_ref, o_ref, kk, vv, ob, lb):
    first = pl.program_id(2) == 0
    qb = A_QBLOCK
    kk[0:qb, :] = kp_ref[...]
    kk[qb:2 * qb, :] = kc_ref[...]
    vv[0:qb, :] = vp_ref[...]
    vv[qb:2 * qb, :] = vc_ref[...]
    scale = A_HEAD_DIM ** -0.5
    qi = lax.broadcasted_iota(jnp.int32, (BLOCK, 2 * BLOCK), 0) + BLOCK
    kj = lax.broadcasted_iota(jnp.int32, (BLOCK, 2 * BLOCK), 1)
    dist = qi - kj
    band = (dist >= 0) & (dist <= BLOCK)
    band_first = band & (kj >= jnp.where(first, BLOCK, 0))
    for b, r in enumerate(A_BRANCH_DILATIONS):
        for rho in range(r):
            for m in range(qb // (BLOCK * r)):
                q0 = rho + BLOCK * r * m
                k0 = qb - BLOCK * r + q0
                if r == 1:
                    qsl, ksl = pl.ds(q0, BLOCK), pl.ds(k0, 2 * BLOCK)
                else:
                    qsl, ksl = pl.ds(q0, BLOCK, stride=r), pl.ds(k0, 2 * BLOCK, stride=r)
                q = q_ref[qsl, :].astype(BF16)
                k = kk[ksl, :].astype(BF16)
                v = vv[ksl, :].astype(BF16)
                s = lax.dot_general(q, k, (((1,), (1,)), ((), ())), preferred_element_type=F32) * scale
                mask = band_first if m == 0 else band
                s = jnp.where(mask, s, NEG_INF)
                mx = jnp.max(s, axis=-1, keepdims=True)
                p = jnp.exp(s - mx)
                den = jnp.sum(p, axis=-1, keepdims=True)
                o = jnp.dot(p.astype(BF16), v, preferred_element_type=F32) / den
                ob[b, qsl, :] = o
                lb[b, qsl, :] = jnp.broadcast_to(mx + jnp.log(den), (BLOCK, LANES))
    lse = [lb[b] for b in range(3)]
    top = jnp.maximum(jnp.maximum(lse[0], lse[1]), lse[2])
    w = [jnp.exp(l - top) for l in lse]
    tot = w[0] + w[1] + w[2]
    o_ref[...] = ((w[0] * ob[0] + w[1] * ob[1] + w[2] * ob[2]) / tot).astype(o_ref.dtype)


def _attn_a_prompt(qkv, n_seq, seq_len):
    h = A_HEADS
    qb = A_QBLOCK
    qkv3 = qkv.reshape(n_seq, seq_len, 3 * h * A_HEAD_DIM)

    def blk(col0, prev):
        if prev:
            return pl.BlockSpec((None, qb, LANES), lambda n, hh, t: (n, jnp.maximum(t - 1, 0), col0 + hh))
        return pl.BlockSpec((None, qb, LANES), lambda n, hh, t: (n, t, col0 + hh))

    out = pl.pallas_call(
        _attn_a_prompt_kernel,
        grid=(n_seq, h, seq_len // qb),
        in_specs=[blk(0, False), blk(h, True), blk(h, False), blk(2 * h, True), blk(2 * h, False)],
        out_specs=pl.BlockSpec((None, qb, LANES), lambda n, hh, t: (n, t, hh)),
        out_shape=jax.ShapeDtypeStruct((n_seq, seq_len, h * A_HEAD_DIM), BF16),
        scratch_shapes=[
            pltpu.VMEM((2 * qb, LANES), F32), pltpu.VMEM((2 * qb, LANES), F32),
            pltpu.VMEM((3, qb, LANES), F32), pltpu.VMEM((3, qb, LANES), F32),
        ],
        compiler_params=_cparams(3),
        name="attn_a_prompt",
    )(qkv3, qkv3, qkv3, qkv3, qkv3)
    return out.reshape(n_seq * seq_len, h * A_HEAD_DIM)


def _attn_b_prompt_kernel(sink_ref, q_ref, kvp_ref, kvc_ref, o_ref, kvs, *, tq):
    first = pl.program_id(1) == 0
    kv_w = 2 * B_KV_HEADS * B_HEAD_DIM
    k_cols = B_KV_HEADS * B_HEAD_DIM
    kvs[0:BLOCK, :] = kvp_ref[...]
    kvs[BLOCK:BLOCK + tq, :] = kvc_ref[...]
    scale = B_HEAD_DIM ** -0.5
    qi = lax.broadcasted_iota(jnp.int32, (BLOCK, 2 * BLOCK), 0) + BLOCK
    kj = lax.broadcasted_iota(jnp.int32, (BLOCK, 2 * BLOCK), 1)
    dist = qi - kj
    band = (dist >= 0) & (dist <= BLOCK)
    lane_half = lax.broadcasted_iota(jnp.int32, (2 * BLOCK, LANES), 1) // B_HEAD_DIM

    def sub_block(sb, carry):
        row0 = pl.multiple_of(sb * BLOCK, BLOCK)
        mask = band & (kj >= jnp.where(jnp.logical_and(first, sb == 0), BLOCK, 0))
        for hk in range(B_KV_HEADS):
            cbk, hh = hk // 2, hk % 2
            kblk = kvs[pl.ds(row0, 2 * BLOCK), cbk * LANES:(cbk + 1) * LANES]
            vblk = kvs[pl.ds(row0, 2 * BLOCK), k_cols + cbk * LANES:k_cols + (cbk + 1) * LANES]
            k_half, v_half = [], []
            for a in range(2):
                ka = kblk if a == hh else pltpu.roll(kblk, B_HEAD_DIM, 1)
                va = vblk if a == hh else pltpu.roll(vblk, B_HEAD_DIM, 1)
                k_half.append(jnp.where(lane_half == a, ka, 0.0).astype(BF16))
                v_half.append(jnp.where(lane_half == a, va, 0.0).astype(BF16))
            for c in range(hk * (B_GROUP // 2), (hk + 1) * (B_GROUP // 2)):
                q2 = q_ref[pl.ds(row0, BLOCK), c * LANES:(c + 1) * LANES].astype(BF16)
                o_pair = jnp.zeros((BLOCK, LANES), F32)
                for a in range(2):
                    sink = sink_ref[2 * c + a]
                    s = lax.dot_general(q2, k_half[a], (((1,), (1,)), ((), ())),
                                        preferred_element_type=F32) * scale
                    s = jnp.where(mask, s, NEG_INF)
                    mx = jnp.maximum(jnp.max(s, axis=-1, keepdims=True), sink)
                    p = jnp.exp(s - mx)
                    den = jnp.sum(p, axis=-1, keepdims=True) + jnp.exp(sink - mx)
                    o_pair = o_pair + jnp.dot(p.astype(BF16), v_half[a], preferred_element_type=F32) / den
                o_ref[pl.ds(row0, BLOCK), c * LANES:(c + 1) * LANES] = o_pair.astype(o_ref.dtype)
        return carry

    lax.fori_loop(0, tq // BLOCK, sub_block, 0)


def _attn_b_prompt(q, kv, sinks, n_seq, seq_len, tq=512):
    d = q.shape[1]
    kv_w = kv.shape[1]
    q3 = q.reshape(n_seq, seq_len, d)
    kv3 = kv.reshape(n_seq, seq_len, kv_w)
    per = tq // BLOCK
    out = pl.pallas_call(
        functools.partial(_attn_b_prompt_kernel, tq=tq),
        grid_spec=pltpu.PrefetchScalarGridSpec(
            num_scalar_prefetch=0,
            grid=(n_seq, seq_len // tq),
            in_specs=[
                pl.BlockSpec(memory_space=pltpu.SMEM),
                pl.BlockSpec((None, tq, d), lambda n, t: (n, t, 0)),
                pl.BlockSpec((None, BLOCK, kv_w), lambda n, t: (n, jnp.maximum(t * per - 1, 0), 0)),
                pl.BlockSpec((None, tq, kv_w), lambda n, t: (n, t, 0)),
            ],
            out_specs=pl.BlockSpec((None, tq, d), lambda n, t: (n, t, 0)),
            scratch_shapes=[pltpu.VMEM((BLOCK + tq, kv_w), F32)],
        ),
        out_shape=jax.ShapeDtypeStruct((n_seq, seq_len, d), BF16),
        compiler_params=_cparams(2),
        name="attn_b_prompt",
    )(sinks, q3, kv3, kv3)
    return out.reshape(n_seq * seq_len, d)


def _attn_a_step_kernel(qkv_ref, k1_ref, k4_ref, k16_ref, v1_ref, v4_ref, v16_ref, o_ref):
    scale = A_HEAD_DIM ** -0.5
    n_br = len(A_BRANCH_DILATIONS)
    q = qkv_ref[0:A_HEADS, :]
    k_new = qkv_ref[A_HEADS:2 * A_HEADS, :]
    v_new = qkv_ref[2 * A_HEADS:3 * A_HEADS, :]
    s_new = jnp.sum(q * k_new, axis=-1, keepdims=True) * scale
    s = [jnp.sum(k_ref[...] * q[None], axis=-1, keepdims=True) * scale
         for k_ref in (k1_ref, k4_ref, k16_ref)]
    top = s_new
    for sb in s:
        top = jnp.maximum(top, jnp.max(sb, axis=0))
    p_new = jnp.exp(s_new - top)
    den = n_br * p_new
    acc = (n_br * p_new) * v_new
    for sb, v_ref in zip(s, (v1_ref, v4_ref, v16_ref)):
        p = jnp.exp(sb - top[None])
        den = den + jnp.sum(p, axis=0)
        acc = acc + jnp.sum(p * v_ref[...], axis=0)
    o_ref[...] = (acc / den).astype(o_ref.dtype)


def _attn_a_step(qkv, cache_k, cache_v, layer):
    n = qkv.shape[0]
    d = A_HEADS * A_HEAD_DIM
    n_layers, _, buf = cache_k.shape[:3]
    assert buf == BLOCK * A_BRANCH_DILATIONS[-1]
    qkv3 = qkv.reshape(n, 3 * A_HEADS, A_HEAD_DIM)

    views, specs = [], []
    for cache in (cache_k, cache_v):
        for r in A_BRANCH_DILATIONS:
            views.append(cache.reshape(n_layers, n, buf // r, r, A_HEADS, A_HEAD_DIM))
            last = buf // (r * BLOCK) - 1
            specs.append(pl.BlockSpec((None, None, BLOCK, None, A_HEADS, A_HEAD_DIM),
                                      lambda i, last=last: (layer, i, last, 0, 0, 0)))
    out = pl.pallas_call(
        _attn_a_step_kernel,
        grid=(n,),
        in_specs=[pl.BlockSpec((None, 3 * A_HEADS, A_HEAD_DIM), lambda i: (i, 0, 0))] + specs,
        out_specs=pl.BlockSpec((None, A_HEADS, A_HEAD_DIM), lambda i: (i, 0, 0)),
        out_shape=jax.ShapeDtypeStruct((n, A_HEADS, A_HEAD_DIM), BF16),
        compiler_params=_cparams(1),
        name="attn_a_step",
    )(qkv3, *views)
    return out.reshape(n, d)


def _attn_b_step_kernel(q_ref, kvn_ref, ck_ref, cv_ref, sink_ref, o_ref):
    scale = B_HEAD_DIM ** -0.5
    for hk in range(B_KV_HEADS):
        rows = slice(hk * B_GROUP, (hk + 1) * B_GROUP)
        qg = q_ref[rows, :]
        k = ck_ref[:, hk, :]
        v = cv_ref[:, hk, :]
        k_new = kvn_ref[hk:hk + 1, :]
        v_new = kvn_ref[B_KV_HEADS + hk:B_KV_HEADS + hk + 1, :]
        sink = sink_ref[rows, :]
        s = lax.dot_general(qg.astype(BF16), k.astype(BF16), (((1,), (1,)), ((), ())),
                            preferred_element_type=F32) * scale
        s_new = jnp.sum(qg * k_new, axis=-1, keepdims=True) * scale
        top = jnp.maximum(jnp.maximum(jnp.max(s, axis=-1, keepdims=True), s_new), sink)
        p = jnp.exp(s - top)
        p_new = jnp.exp(s_new - top)
        den = jnp.sum(p, axis=-1, keepdims=True) + p_new + jnp.exp(sink - top)
        acc = jnp.dot(p.astype(BF16), v.astype(BF16), preferred_element_type=F32) + p_new * v_new
        o_ref[rows, :] = acc / den


def _attn_b_step(q, kv_new, cache_k, cache_v, sinks):
    n = q.shape[0]
    win = cache_k.shape[1]
    assert win == BLOCK
    q3 = q.reshape(n, B_HEADS, B_HEAD_DIM)
    kvn = kv_new.reshape(n, 2 * B_KV_HEADS, B_HEAD_DIM)
    cspec = pl.BlockSpec((None, win, B_KV_HEADS, B_HEAD_DIM), lambda i: (i, 0, 0, 0))
    out = pl.pallas_call(
        _attn_b_step_kernel,
        grid=(n,),
        in_specs=[
            pl.BlockSpec((None, B_HEADS, B_HEAD_DIM), lambda i: (i, 0, 0)),
            pl.BlockSpec((None, 2 * B_KV_HEADS, B_HEAD_DIM), lambda i: (i, 0, 0)),
            cspec, cspec,
            pl.BlockSpec((B_HEADS, 1), lambda i: (0, 0)),
        ],
        out_specs=pl.BlockSpec((None, B_HEADS, B_HEAD_DIM), lambda i: (i, 0, 0)),
        out_shape=jax.ShapeDtypeStruct((n, B_HEADS, B_HEAD_DIM), F32),
        compiler_params=_cparams(1),
        name="attn_b_step",
    )(q3, kvn, cache_k, cache_v, sinks.reshape(B_HEADS, 1))
    return out.reshape(n, B_HEADS * B_HEAD_DIM)


def _shift_b_kernel(kvn_ref, ck_ref, cv_ref, nk_ref, nv_ref):
    win = ck_ref.shape[0]
    nk_ref[0:win - 1] = ck_ref[1:win]
    nv_ref[0:win - 1] = cv_ref[1:win]
    nk_ref[win - 1] = kvn_ref[0:B_KV_HEADS, :]
    nv_ref[win - 1] = kvn_ref[B_KV_HEADS:2 * B_KV_HEADS, :]


def _shift_b_cache(kv_new, cache_k, cache_v):
    n, win = cache_k.shape[:2]
    kvn = kv_new.reshape(n, 2 * B_KV_HEADS, B_HEAD_DIM)
    cspec = pl.BlockSpec((None, win, B_KV_HEADS, B_HEAD_DIM), lambda i: (i, 0, 0, 0))
    return pl.pallas_call(
        _shift_b_kernel,
        grid=(n,),
        in_specs=[pl.BlockSpec((None, 2 * B_KV_HEADS, B_HEAD_DIM), lambda i: (i, 0, 0)), cspec, cspec],
        out_specs=[cspec, cspec],
        out_shape=[jax.ShapeDtypeStruct(cache_k.shape, F32), jax.ShapeDtypeStruct(cache_v.shape, F32)],
        compiler_params=_cparams(1),
        name="shift_b_cache",
    )(kvn, cache_k, cache_v)


SHIFT_ROWS = 512


def _shift_a_kernel(kc_ref, kx_ref, kn_ref, vc_ref, vx_ref, vn_ref, nk_ref, nv_ref):
    t = pl.program_id(2)
    last = pl.num_programs(2) - 1
    rows = nk_ref.shape[0]
    for cur, nxt, new, out in ((kc_ref, kx_ref, kn_ref, nk_ref), (vc_ref, vx_ref, vn_ref, nv_ref)):
        out[0:rows - 1] = cur[1:rows]

        @pl.when(t < last)
        def _(nxt=nxt, out=out):
            out[rows - 1] = nxt[0]

        @pl.when(t == last)
        def _(new=new, out=out):
            out[rows - 1] = new[0]


def _shift_a_cache(qkv_layers, cache_k, cache_v):
    n_layers, n_seq, buf = cache_k.shape[:3]
    d = A_HEADS * A_HEAD_DIM
    rows = SHIFT_ROWS
    k_new = jnp.stack([q[:, d:2 * d] for q in qkv_layers]).reshape(n_layers, n_seq, 1, A_HEADS, A_HEAD_DIM)
    v_new = jnp.stack([q[:, 2 * d:] for q in qkv_layers]).reshape(n_layers, n_seq, 1, A_HEADS, A_HEAD_DIM)
    cur = pl.BlockSpec((None, None, rows, A_HEADS, A_HEAD_DIM), lambda l, n, t: (l, n, t, 0, 0))
    nxt = pl.BlockSpec((None, None, 1, A_HEADS, A_HEAD_DIM),
                       lambda l, n, t: (l, n, jnp.minimum((t + 1) * rows, buf - 1), 0, 0))
    new = pl.BlockSpec((None, None, 1, A_HEADS, A_HEAD_DIM), lambda l, n, t: (l, n, 0, 0, 0))
    return pl.pallas_call(
        _shift_a_kernel,
        grid=(n_layers, n_seq, buf // rows),
        in_specs=[cur, nxt, new, cur, nxt, new],
        out_specs=[cur, cur],
        out_shape=[jax.ShapeDtypeStruct(cache_k.shape, F32), jax.ShapeDtypeStruct(cache_v.shape, F32)],
        compiler_params=_cparams(3),
        name="shift_a_cache",
    )(cache_k, cache_k, k_new, cache_v, cache_v, v_new)


def _route_tile(x_ref, sh_ref, sc_ref, rw_ref, rb_ref, h_ref, e_ref, w_ref, r_ref, cnt_ref):
    h = _adaln(x_ref[...], sh_ref[...], sc_ref[...])
    h_ref[...] = h
    logits = lax.dot_general(rw_ref[...], h, (((1,), (1,)), ((), ())),
                             precision=lax.Precision.HIGHEST, preferred_element_type=F32)
    scores = jax.nn.sigmoid(logits)
    sel = scores + rb_ref[...]
    tm = sel.shape[1]
    pos = lax.broadcasted_iota(jnp.int32, (EXPERTS_PER_GROUP, tm), 0)

    def top2(v):
        m1 = jnp.max(v, axis=0, keepdims=True)
        i1 = jnp.min(jnp.where(v == m1, pos, EXPERTS_PER_GROUP), axis=0, keepdims=True)
        v2 = jnp.where(pos == i1, -jnp.inf, v)
        m2 = jnp.max(v2, axis=0, keepdims=True)
        i2 = jnp.min(jnp.where(v2 == m2, pos, EXPERTS_PER_GROUP), axis=0, keepdims=True)
        return m1 + m2, i1, i2

    best, e1, e2 = None, None, None
    for g in range(N_GROUPS):
        gs, i1, i2 = top2(sel[g * EXPERTS_PER_GROUP:(g + 1) * EXPERTS_PER_GROUP, :])
        i1 = i1 + g * EXPERTS_PER_GROUP
        i2 = i2 + g * EXPERTS_PER_GROUP
        if g == 0:
            best, e1, e2 = gs, i1, i2
        else:
            take = gs > best
            best = jnp.where(take, gs, best)
            e1 = jnp.where(take, i1, e1)
            e2 = jnp.where(take, i2, e2)
    eid = lax.broadcasted_iota(jnp.int32, (N_EXPERTS, tm), 0)
    w1 = jnp.sum(jnp.where(eid == e1, scores, 0.0), axis=0, keepdims=True)
    w2 = jnp.sum(jnp.where(eid == e2, scores, 0.0), axis=0, keepdims=True)
    tot = w1 + w2
    e_ref[0:1, :] = e1
    e_ref[1:2, :] = e2
    w_ref[0:1, :] = w1 / tot
    w_ref[1:2, :] = w2 / tot
    own1, own2 = eid == e1, eid == e2
    earlier = (lax.broadcasted_iota(jnp.int32, (tm, tm), 0)
               < lax.broadcasted_iota(jnp.int32, (tm, tm), 1)).astype(BF16)
    pre1 = jnp.dot(own1.astype(BF16), earlier, preferred_element_type=F32)
    pre2 = jnp.dot(own2.astype(BF16), earlier, preferred_element_type=F32)
    n1 = jnp.sum(own1.astype(F32), axis=1, keepdims=True)
    n2 = jnp.sum(own2.astype(F32), axis=1, keepdims=True)
    base = cnt_ref[...]
    r1 = jnp.sum(jnp.where(own1, pre1 + base, 0.0), axis=0, keepdims=True)
    r2 = jnp.sum(jnp.where(own2, pre2 + (base + n1), 0.0), axis=0, keepdims=True)
    r_ref[0:1, :] = r1.astype(jnp.int32)
    r_ref[1:2, :] = r2.astype(jnp.int32)
    cnt_ref[...] = base + n1 + n2


def _router_kernel(*refs, n_real, aliased):
    ins, outs = (refs[:7], refs[7:]) if aliased else (refs[:6], refs[6:])
    x_ref, sh_ref, sc_ref, rw_ref, rb_ref, c0_ref = ins[:6]
    h_ref, e_ref, w_ref, r_ref, cnt_ref = outs
    i = pl.program_id(0)

    @pl.when(i == 0)
    def _():
        cnt_ref[...] = c0_ref[...]

    @pl.when(i < n_real)
    def _():
        _route_tile(x_ref, sh_ref, sc_ref, rw_ref, rb_ref, h_ref, e_ref, w_ref, r_ref, cnt_ref)

    @pl.when(i >= n_real)
    def _():
        h_ref[...] = jnp.zeros_like(h_ref)


def _router(x, mod, layer, router_w_t, router_bias, counts, tm, m_total, row0, h_all, name):
    m, d = x.shape
    blk0 = row0 // tm
    n_real = m // tm
    n_steps = n_real if h_all is not None else -(-m_total // tm)
    real = lambda i: jnp.minimum(i, n_real - 1)
    in_specs = [
        pl.BlockSpec((tm, d), lambda i: (real(i), 0)),
        mod.spec(layer, 3, D_MODEL, tm, False),
        mod.spec(layer, 4, D_MODEL, tm, False),
        pl.BlockSpec((N_EXPERTS, d), lambda i: (0, 0)),
        pl.BlockSpec((N_EXPERTS, 1), lambda i: (0, 0)),
        pl.BlockSpec((N_EXPERTS, 1), lambda i: (0, 0)),
    ]
    args = [x, mod.arr, mod.arr, router_w_t, router_bias.reshape(N_EXPERTS, 1), counts]
    if h_all is not None:
        in_specs.append(pl.BlockSpec(memory_space=pl.ANY))
        args.append(h_all)
    return pl.pallas_call(
        functools.partial(_router_kernel, n_real=n_real, aliased=h_all is not None),
        grid=(n_steps,),
        in_specs=in_specs,
        out_specs=[
            pl.BlockSpec((tm, d), lambda i: (blk0 + i, 0)),
            pl.BlockSpec((2, tm), lambda i: (0, real(i))),
            pl.BlockSpec((2, tm), lambda i: (0, real(i))),
            pl.BlockSpec((2, tm), lambda i: (0, real(i))),
            pl.BlockSpec((N_EXPERTS, 1), lambda i: (0, 0)),
        ],
        out_shape=[
            jax.ShapeDtypeStruct((m_total, d), F32),
            jax.ShapeDtypeStruct((2, m), jnp.int32),
            jax.ShapeDtypeStruct((2, m), F32),
            jax.ShapeDtypeStruct((2, m), jnp.int32),
            jax.ShapeDtypeStruct((N_EXPERTS, 1), F32),
        ],
        input_output_aliases={} if h_all is None else {len(args) - 1: 0},
        compiler_params=_cparams(1),
        name=name,
    )(*args)


def _row_gather(idx_of_row, src_ref, dst_ref, sem, n_rows):
    def copy(r):
        return pltpu.make_async_copy(src_ref.at[pl.ds(idx_of_row(r), 1)], dst_ref.at[pl.ds(r, 1)], sem)

    def start():
        def body(r, carry):
            copy(r).start()
            return carry
        lax.fori_loop(0, n_rows, body, 0, unroll=8)

    def wait():
        def body(r, carry):
            copy(r).wait()
            return carry
        lax.fori_loop(0, n_rows, body, 0, unroll=8)

    return start, wait


def _gather_kernel(e_ref, r_ref, rs_ref, cnt_ref, vt_ref, nv_ref, h_ref, xs0_ref, xs_ref, buf, sem, src, *, m_t):
    del xs0_ref
    i = pl.program_id(0)
    nv = nv_ref[0]
    cur = i % 2

    def tile(t, b):
        base = vt_ref[t] * MOE_SUB
        return _row_gather(lambda r: src[base + r], h_ref, buf.at[b], sem.at[b], MOE_SUB)

    @pl.when(i == 0)
    def _():
        def clear(s, carry):
            src[s] = jnp.int32(0)
            return carry

        for e in range(N_EXPERTS):
            used = rs_ref[e] + cnt_ref[e]
            lax.fori_loop(used, rs_ref[e] + (cnt_ref[e] + MOE_SUB - 1) // MOE_SUB * MOE_SUB, clear, 0)
        for k in range(2):
            def fill(t, carry, k=k):
                p = k * m_t + t
                src[rs_ref[e_ref[p]] + r_ref[p]] = t
                return carry

            lax.fori_loop(0, m_t, fill, 0, unroll=8)
        tile(0, 0)[0]()

    @pl.when(i < nv)
    def _():
        tile(i, cur)[1]()

        @pl.when(i + 1 < nv)
        def _():
            tile(i + 1, 1 - cur)[0]()

        xs_ref[...] = buf[cur].astype(BF16)


def _dispatch(e_flat, r_flat, region_start, counts, live_tiles, n_live, h_all, n_slots, max_live):
    m_t, d = h_all.shape
    return pl.pallas_call(
        functools.partial(_gather_kernel, m_t=m_t),
        grid_spec=pltpu.PrefetchScalarGridSpec(
            num_scalar_prefetch=6,
            grid=(max_live,),
            in_specs=[pl.BlockSpec(memory_space=pl.ANY), pl.BlockSpec(memory_space=pl.ANY)],
            out_specs=pl.BlockSpec((MOE_SUB, d), lambda i, e, r, rs, cnt, vt, nv: (vt[jnp.minimum(i, nv[0] - 1)], 0)),
            scratch_shapes=[pltpu.VMEM((2, MOE_SUB, d), F32), pltpu.SemaphoreType.DMA((2,)),
                            pltpu.SMEM((n_slots,), jnp.int32)],
        ),
        out_shape=jax.ShapeDtypeStruct((n_slots, d), BF16),
        input_output_aliases={7: 0},
        compiler_params=_cparams(1),
        name="moe_dispatch",
    )(e_flat, r_flat, region_start, counts, live_tiles, n_live, h_all, jnp.zeros((n_slots, d), BF16))


def _experts_kernel(ce_ref, cs_ref, nu_ref, xs_ref, wg_ref, wu_ref, wd_ref, y_ref):
    c = pl.program_id(0)
    j = pl.program_id(1)

    @pl.when(c < nu_ref[0])
    def _():
        wg = wg_ref[...].astype(BF16)
        wu = wu_ref[...].astype(BF16)
        wd = wd_ref[...].astype(BF16)
        n_live = cs_ref[c]
        d = y_ref.shape[1]
        for n in range(1, MOE_CHUNK_SUBS + 1):
            live = n * MOE_SUB

            @pl.when(n_live == n)
            def _(live=live):
                x = xs_ref[0:live, :]
                a = jnp.dot(x, wg, preferred_element_type=F32)
                u = jnp.dot(x, wu, preferred_element_type=F32)
                hidden = (a * jax.nn.sigmoid(a) * u).astype(BF16)
                for c0 in range(0, d, MOE_DOWN_COLS):
                    cols = slice(c0, c0 + MOE_DOWN_COLS)
                    part = jnp.dot(hidden, wd[:, cols], preferred_element_type=F32)

                    @pl.when(j == 0)
                    def _(part=part, cols=cols):
                        y_ref[0:live, cols] = part

                    @pl.when(j > 0)
                    def _(part=part, cols=cols):
                        y_ref[0:live, cols] += part

                if live < y_ref.shape[0]:
                    @pl.when(j == 0)
                    def _():
                        y_ref[live:, :] = jnp.zeros((y_ref.shape[0] - live, d), F32)

    @pl.when(jnp.logical_and(c >= nu_ref[0], j == 0))
    def _():
        y_ref[...] = jnp.zeros_like(y_ref)


def _experts(chunk_expert, chunk_subs, n_used, xs, w_gate, w_up, w_down, layer):
    n_slots, d = xs.shape
    f = w_gate.shape[-1]
    tf = MOE_TF
    nf = f // tf
    ch = MOE_SUB * MOE_CHUNK_SUBS

    def row(c, nu):
        return jnp.minimum(c, nu[0] - 1)

    def col(c, j, nu):
        return jnp.where(c < nu[0], j, nf - 1)

    return pl.pallas_call(
        _experts_kernel,
        grid_spec=pltpu.PrefetchScalarGridSpec(
            num_scalar_prefetch=3,
            grid=(n_slots // ch, nf),
            in_specs=[
                pl.BlockSpec((ch, d), lambda c, j, ce, cs, nu: (row(c, nu), 0)),
                pl.BlockSpec((None, None, d, tf), lambda c, j, ce, cs, nu: (layer, ce[c], 0, col(c, j, nu))),
                pl.BlockSpec((None, None, d, tf), lambda c, j, ce, cs, nu: (layer, ce[c], 0, col(c, j, nu))),
                pl.BlockSpec((None, None, tf, d), lambda c, j, ce, cs, nu: (layer, ce[c], col(c, j, nu), 0)),
            ],
            out_specs=pl.BlockSpec((ch, d), lambda c, j, ce, cs, nu: (c, 0)),
        ),
        out_shape=jax.ShapeDtypeStruct((n_slots, d), F32),
        compiler_params=_cparams(2),
        name="moe_experts",
    )(chunk_expert, chunk_subs, n_used, xs, w_gate, w_up, w_down)


def _combine_kernel(e_ref, r_ref, rs_ref, y_ref, x_ref, w_ref, g_ref, o_ref, buf, sem, *, tm, tok0, m_t):
    i = pl.program_id(0)
    n = pl.num_programs(0)
    cur = i % 2

    def tile(t, b, k):
        base = k * m_t + tok0 + t * tm
        return _row_gather(lambda r: rs_ref[e_ref[base + r]] + r_ref[base + r], y_ref, buf.at[b, k],
                           sem.at[b], tm)

    @pl.when(i == 0)
    def _():
        tile(0, 0, 0)[0]()
        tile(0, 0, 1)[0]()

    tile(i, cur, 0)[1]()
    tile(i, cur, 1)[1]()

    @pl.when(i + 1 < n)
    def _():
        tile(i + 1, 1 - cur, 0)[0]()
        tile(i + 1, 1 - cur, 1)[0]()

    w = w_ref[...]
    moe = w[:, 0:1] * buf[cur, 0] + w[:, 1:2] * buf[cur, 1]
    o_ref[...] = x_ref[...] + g_ref[...] * moe


def _combine(e_flat, r_flat, region_start, y, x, w_t, mod, layer, tok0, m_t, tm, name):
    m, d = x.shape
    return pl.pallas_call(
        functools.partial(_combine_kernel, tm=tm, tok0=tok0, m_t=m_t),
        grid_spec=pltpu.PrefetchScalarGridSpec(
            num_scalar_prefetch=3,
            grid=(m // tm,),
            in_specs=[
                pl.BlockSpec(memory_space=pl.ANY),
                pl.BlockSpec((tm, d), lambda i, *_: (i, 0)),
                pl.BlockSpec((tm, 2), lambda i, *_: (tok0 // tm + i, 0)),
                mod.spec(layer, 5, D_MODEL, tm, False),
            ],
            out_specs=pl.BlockSpec((tm, d), lambda i, *_: (i, 0)),
            scratch_shapes=[pltpu.VMEM((2, 2, tm, d), F32), pltpu.SemaphoreType.DMA((2,))],
        ),
        out_shape=jax.ShapeDtypeStruct((m, d), F32),
        compiler_params=_cparams(1),
        name=name,
    )(e_flat, r_flat, region_start, y, x, w_t, mod.arr)


def _slot_plan(counts, max_chunks, max_live):
    sub, per = MOE_SUB, MOE_CHUNK_SUBS
    n_sub = (counts + sub - 1) // sub
    n_chunk = (n_sub + per - 1) // per
    chunk_end = jnp.cumsum(n_chunk)
    chunk_base = chunk_end - n_chunk
    n_used = chunk_end[-1]
    c = jnp.minimum(jnp.arange(max_chunks, dtype=jnp.int32), n_used - 1)
    chunk_expert = jnp.minimum(jnp.searchsorted(chunk_end, c, side="right"), N_EXPERTS - 1)
    chunk_subs = jnp.clip(n_sub[chunk_expert] - (c - chunk_base[chunk_expert]) * per, 0, per)
    sub_end = jnp.cumsum(n_sub)
    n_live = sub_end[-1]
    t = jnp.minimum(jnp.arange(max_live, dtype=jnp.int32), n_live - 1)
    t_expert = jnp.minimum(jnp.searchsorted(sub_end, t, side="right"), N_EXPERTS - 1)
    live_tiles = chunk_base[t_expert] * per + (t - (sub_end - n_sub)[t_expert])
    i32 = lambda a: a.astype(jnp.int32)
    return (i32(chunk_base * (per * sub)), i32(live_tiles), i32(n_live).reshape(1), i32(chunk_expert),
            i32(chunk_subs), i32(n_used).reshape(1))


def _moe_layer(x_p, x_s, mod_p, mod_s, layer, router_w_t, router_bias, w_gate, w_up, w_down):
    m_p, d = x_p.shape
    m_s = x_s.shape[0]
    m_t = m_p + m_s
    sub, per = MOE_SUB, MOE_CHUNK_SUBS
    max_live = (2 * m_t) // sub + N_EXPERTS
    max_chunks = max_live // per + N_EXPERTS
    n_slots = max_chunks * per * sub
    zero = jnp.zeros((N_EXPERTS, 1), F32)
    h_all, e_p, w_p, r_p, cnt = _router(x_p, mod_p, layer, router_w_t, router_bias, zero, 512, m_t, 0, None,
                                        "router_prompt")
    h_all, e_s, w_s, r_s, cnt = _router(x_s, mod_s, layer, router_w_t, router_bias, cnt, m_s, m_t, m_p, h_all,
                                        "router_sample")
    e_flat = jnp.concatenate([e_p, e_s], axis=1).reshape(-1)
    r_flat = jnp.concatenate([r_p, r_s], axis=1).reshape(-1)
    w_t = jnp.concatenate([w_p, w_s], axis=1).T
    counts = cnt[:, 0].astype(jnp.int32)
    region_start, live_tiles, n_live, chunk_expert, chunk_subs, n_used = _slot_plan(counts, max_chunks, max_live)
    xs = _dispatch(e_flat, r_flat, region_start, counts, live_tiles, n_live, h_all, n_slots, max_live)
    y = _experts(chunk_expert, chunk_subs, n_used, xs, w_gate, w_up, w_down, layer)
    x_p = _combine(e_flat, r_flat, region_start, y, x_p, w_t, mod_p, layer, 0, m_t, 256, "combine_prompt")
    x_s = _combine(e_flat, r_flat, region_start, y, x_s, w_t, mod_s, layer, m_p, m_t, m_s, "combine_sample")
    return x_p, x_s


def kernel(x_prompt, x_sample, cache_a_k, cache_a_v, cache_b_k, cache_b_v, c_prompt, c_sample, a_w_qkv, a_q_gain, a_k_gain, a_w_o, b_w_q, b_q_gain, b_sinks, b_w_o, kv_w, kv_k_gain, kv_mod_w, kv_mod_b, mod_w, mod_b, router_w, router_bias, moe_w_gate, moe_w_up, moe_w_down):
    n_seq, seq_len, d = x_prompt.shape
    n_smp = x_sample.shape[0]
    m_p = n_seq * seq_len
    x_p = x_prompt.reshape(m_p, d)
    x_s = x_sample.reshape(n_smp, d)

    n_rows = -(-(n_smp + n_seq) // 8) * 8
    c_all = jnp.concatenate([c_sample, c_prompt, jnp.zeros((n_rows - n_smp - n_seq, d), F32)], axis=0)
    mod = _modulation(c_all, mod_w, mod_b)
    kv_mod = _modulation(c_all, kv_mod_w[None], kv_mod_b[None])
    mod_p = _Mod(mod, n_smp, False, seq_len)
    mod_s = _Mod(mod, n_smp, True)
    kvmod_p = _Mod(kv_mod, n_smp, False, seq_len)
    kvmod_s = _Mod(kv_mod, n_smp, True)

    pos_p = jnp.arange(seq_len, dtype=jnp.int32)
    pos_s = jnp.full((n_smp,), PAST_LEN, dtype=jnp.int32)
    tab_a_p, tab_a_s = _rope_tables(pos_p, A_HEAD_DIM), _rope_tables(pos_s, A_HEAD_DIM)
    tab_b_p, tab_b_s = _rope_tables(pos_p, B_HEAD_DIM), _rope_tables(pos_s, B_HEAD_DIM)

    router_w_t = router_w.T
    a_gains = jnp.stack([a_q_gain, a_k_gain], axis=1).reshape(N_A_LAYERS, 2, 1, A_HEAD_DIM)
    b_gains = jnp.concatenate([b_q_gain, b_q_gain], axis=-1).reshape(-1, 1, 1, LANES)
    kv_gain = jnp.concatenate([kv_k_gain, kv_k_gain]).reshape(1, 1, LANES)
    da = A_HEADS * A_HEAD_DIM

    tm_p, tn = 1024, 512
    qkv_s_layers, ak_p, av_p = [], [], []
    kv_p = kv_s = None
    for layer in range(DEPTH):
        if layer < N_A_LAYERS:
            qkv_p = _adaln_matmul(x_p, mod_p, layer, (0, 1), a_w_qkv, layer, a_gains[layer], da // tn,
                                  2 * da // tn, A_HEAD_DIM, tab_a_p, tm_p, tn, "qkv_prompt")
            qkv_s = _adaln_matmul(x_s, mod_s, layer, (0, 1), a_w_qkv, layer, a_gains[layer], da // tn,
                                  2 * da // tn, A_HEAD_DIM, tab_a_s, n_smp, tn, "qkv_sample")
            o_p = _attn_a_prompt(qkv_p, n_seq, seq_len)
            o_s = _attn_a_step(qkv_s, cache_a_k, cache_a_v, layer)
            qkv_s_layers.append(qkv_s)
            keep = min(BLOCK * A_BRANCH_DILATIONS[-1], seq_len)
            qkv3 = qkv_p.reshape(n_seq, seq_len, 3 * da)
            ak_p.append(qkv3[:, seq_len - keep:, da:2 * da].reshape(n_seq, keep, A_HEADS, A_HEAD_DIM))
            av_p.append(qkv3[:, seq_len - keep:, 2 * da:].reshape(n_seq, keep, A_HEADS, A_HEAD_DIM))
            w_o, w_o_layer = a_w_o, layer
        else:
            jb = layer - N_A_LAYERS
            if layer == N_A_LAYERS:
                kvn = 2 * B_KV_HEADS * B_HEAD_DIM
                kv_p = _adaln_matmul(x_p, kvmod_p, 0, (0, 1), kv_w[None], 0, kv_gain, 1, 1, B_HEAD_DIM,
                                     tab_b_p, tm_p, kvn // 2, "kv_prompt")
                kv_s = _adaln_matmul(x_s, kvmod_s, 0, (0, 1), kv_w[None], 0, kv_gain, 1, 1, B_HEAD_DIM,
                                     tab_b_s, n_smp, kvn // 2, "kv_sample")
            q_p = _adaln_matmul(x_p, mod_p, layer, (0, 1), b_w_q, jb, b_gains[jb], d // tn, d // tn,
                                B_HEAD_DIM, tab_b_p, tm_p, tn, "q_prompt")
            q_s = _adaln_matmul(x_s, mod_s, layer, (0, 1), b_w_q, jb, b_gains[jb], d // tn, d // tn,
                                B_HEAD_DIM, tab_b_s, n_smp, tn, "q_sample")
            o_p = _attn_b_prompt(q_p, kv_p, b_sinks[jb], n_seq, seq_len)
            o_s = _attn_b_step(q_s, kv_s, cache_b_k, cache_b_v, b_sinks[jb])
            w_o, w_o_layer = b_w_o, jb
        x_p = _out_proj(o_p, w_o, w_o_layer, x_p, mod_p, layer, 2, tm_p, tn, "oproj_prompt")
        x_s = _out_proj(o_s, w_o, w_o_layer, x_s, mod_s, layer, 2, n_smp, tn, "oproj_sample")
        x_p, x_s = _moe_layer(x_p, x_s, mod_p, mod_s, layer, router_w_t, router_bias,
                              moe_w_gate, moe_w_up, moe_w_down)

    ak_s, av_s = _shift_a_cache(qkv_s_layers, cache_a_k, cache_a_v)
    bk_s, bv_s = _shift_b_cache(kv_s, cache_b_k, cache_b_v)
    keep_b = min(BLOCK, seq_len)
    kvh = B_KV_HEADS * B_HEAD_DIM
    kv_tail = kv_p.reshape(n_seq, seq_len, 2 * kvh)[:, seq_len - keep_b:]
    bk_p = kv_tail[..., :kvh].reshape(n_seq, keep_b, B_KV_HEADS, B_HEAD_DIM)
    bv_p = kv_tail[..., kvh:].reshape(n_seq, keep_b, B_KV_HEADS, B_HEAD_DIM)
    return (x_p.reshape(n_seq, seq_len, d), x_s.reshape(n_smp, 1, d),
            jnp.stack(ak_p), jnp.stack(av_p), bk_p, bv_p, ak_s, av_s, bk_s, bv_s)
```

```python
import functools
import math

import jax
import jax.numpy as jnp
from jax import lax
from jax.experimental import pallas as pl
from jax.experimental.pallas import tpu as pltpu

F32 = jnp.float32
BF16 = jnp.bfloat16

D_MODEL = 2048
DEPTH = 4
N_A_LAYERS = DEPTH // 2
PAST_LEN = 16384
A_HEADS = 16
A_HEAD_DIM = 128
A_BRANCH_DILATIONS = (1, 4, 16)
B_HEADS = 32
B_HEAD_DIM = 64
B_KV_HEADS = 4
B_GROUP = B_HEADS // B_KV_HEADS
ROPE_THETA = 500000.0
N_EXPERTS = 16
N_GROUPS = 4
EXPERTS_PER_GROUP = N_EXPERTS // N_GROUPS
D_EXPERT = D_MODEL // 2
BLOCK = 128
LANES = 128
NORM_ROWS = 256
NORM_EPS = 1e-6
NEG_INF = -1e30

A_QBLOCK = 2048
MOE_SUB = 256
MOE_CHUNK_SUBS = 5
MOE_TF = 256
MOE_DOWN_COLS = 512
VMEM_LIMIT = 56 * 1024 * 1024


def _cparams(n_axes, vmem=VMEM_LIMIT):
    return pltpu.CompilerParams(dimension_semantics=("arbitrary",) * n_axes, vmem_limit_bytes=vmem)


def _mod_kernel(c_ref, w_ref, b_ref, o_ref):
    c = c_ref[...]
    h = (c * jax.nn.sigmoid(c)).astype(BF16)
    o_ref[...] = jnp.dot(h, w_ref[...].astype(BF16), preferred_element_type=F32) + b_ref[...]


def _modulation(c_all, w, b, tn=1024):
    n_layers, d, n = w.shape
    r = c_all.shape[0]
    return pl.pallas_call(
        _mod_kernel,
        grid=(n_layers, n // tn),
        in_specs=[
            pl.BlockSpec((r, d), lambda l, j: (0, 0)),
            pl.BlockSpec((None, d, tn), lambda l, j: (l, 0, j)),
            pl.BlockSpec((None, 1, tn), lambda l, j: (l, 0, j)),
        ],
        out_specs=pl.BlockSpec((None, r, tn), lambda l, j: (l, 0, j)),
        out_shape=jax.ShapeDtypeStruct((n_layers, r, n), F32),
        compiler_params=_cparams(2),
        name="modulation",
    )(c_all, w, b.reshape(n_layers, 1, n))


class _Mod:
    def __init__(self, mod, n_sample, per_row, rows_per_seq=None):
        self.per_row = per_row
        self.n_sample = n_sample
        self.rows_per_seq = rows_per_seq
        n_layers, r, n = mod.shape
        self.arr = mod if per_row else mod.reshape(n_layers, r, 1, n)

    def spec(self, layer, chunk, width, tm, col_from_j):
        per = D_MODEL // width

        def col(rest):
            return chunk * per + (rest[0] if col_from_j else 0)

        if self.per_row:
            return pl.BlockSpec((None, self.n_sample, width), lambda i, *rest: (layer, 0, col(rest)))
        tiles_per_seq = self.rows_per_seq // tm
        base = self.n_sample
        return pl.BlockSpec((None, None, 1, width),
                            lambda i, *rest: (layer, base + i // tiles_per_seq, 0, col(rest)))


def _rope_tables(pos, head_dim):
    rot = head_dim // 4
    half = rot // 2
    inv = jnp.exp(jnp.arange(half, dtype=F32) * (-math.log(ROPE_THETA) / half))
    ang = pos.astype(F32)[:, None] * inv[None, :]
    cos, sin = jnp.cos(ang), jnp.sin(ang)
    lane = jnp.arange(LANES) % head_dim
    idx = lane % half
    c = jnp.where(lane[None, :] < rot, cos[:, idx], 1.0)
    s1 = jnp.where(((lane >= half) & (lane < rot))[None, :], sin[:, idx], 0.0)
    s2 = jnp.where((lane < half)[None, :], -sin[:, idx], 0.0)
    return c.astype(F32), s1.astype(F32), s2.astype(F32)


def _adaln(x, shift, scale):
    r = lax.rsqrt(jnp.mean(x * x, axis=-1, keepdims=True) + NORM_EPS)
    return x * r * (1.0 + scale) + shift


def _adaln_mm_kernel(x_ref, sh_ref, sc_ref, w_ref, g_ref, c_ref, s1_ref, s2_ref, o_ref, h_ref,
                     *, head_dim, n_norm, n_tiles, tn, row_groups):
    j = pl.program_id(1)

    @pl.when(j == 0)
    def _():
        h_ref[...] = _adaln(x_ref[...], sh_ref[...], sc_ref[...]).astype(BF16)

    def plain():
        o_ref[...] = jnp.dot(h_ref[...], w_ref[...].astype(BF16), preferred_element_type=F32)

    def normed():
        w16 = w_ref[...].astype(BF16)
        rows_per = h_ref.shape[0] // row_groups
        for rg in range(row_groups):
            rows = slice(rg * rows_per, (rg + 1) * rows_per)
            acc = jnp.dot(h_ref[rows, :], w16, preferred_element_type=F32)
            normed_rows(acc, rows)

    def normed_rows(acc, rows):
        half = head_dim // 8
        gain = g_ref[...]
        c, s1, s2 = c_ref[rows, :], s1_ref[rows, :], s2_ref[rows, :]
        for cb in range(tn // LANES):
            a = acc[:, cb * LANES:(cb + 1) * LANES]
            sq = a * a
            if head_dim == LANES:
                ms = jnp.mean(sq, axis=-1, keepdims=True)
            else:
                lo = lax.broadcasted_iota(jnp.int32, sq.shape, 1) < head_dim
                s_lo = jnp.sum(jnp.where(lo, sq, 0.0), axis=-1, keepdims=True)
                s_hi = jnp.sum(jnp.where(lo, 0.0, sq), axis=-1, keepdims=True)
                ms = jnp.where(lo, s_lo, s_hi) * (1.0 / head_dim)
            a = a * lax.rsqrt(ms + NORM_EPS) * gain
            a = a * c + pltpu.roll(a, half, 1) * s1 + pltpu.roll(a, LANES - half, 1) * s2
            o_ref[rows, cb * LANES:(cb + 1) * LANES] = a

    if n_norm >= n_tiles:
        normed()
    elif n_norm == 0:
        plain()
    else:
        pl.when(j < n_norm)(normed)
        pl.when(j >= n_norm)(plain)


def _adaln_matmul(x, mod, layer, chunks, w, w_layer, gains, tiles_per_gain, n_norm, head_dim, tables,
                  tm, tn, name):
    m, d = x.shape
    n = w.shape[-1]
    n_tiles = n // tn
    table_rows = tables[0].shape[0]
    table_tiles = table_rows // tm
    n_gains = gains.shape[0]
    tab_spec = pl.BlockSpec((tm, LANES), lambda i, j: (i % table_tiles, 0))
    kern = functools.partial(_adaln_mm_kernel, head_dim=head_dim, n_norm=n_norm, n_tiles=n_tiles, tn=tn,
                             row_groups=max(1, tm // NORM_ROWS))
    return pl.pallas_call(
        kern,
        grid=(m // tm, n_tiles),
        in_specs=[
            pl.BlockSpec((tm, d), lambda i, j: (i, 0)),
            mod.spec(layer, chunks[0], D_MODEL, tm, False),
            mod.spec(layer, chunks[1], D_MODEL, tm, False),
            pl.BlockSpec((None, d, tn), lambda i, j: (w_layer, 0, j)),
            pl.BlockSpec((None, 1, LANES), lambda i, j: (jnp.minimum(j // tiles_per_gain, n_gains - 1), 0, 0)),
            tab_spec, tab_spec, tab_spec,
        ],
        out_specs=pl.BlockSpec((tm, tn), lambda i, j: (i, j)),
        out_shape=jax.ShapeDtypeStruct((m, n), F32),
        scratch_shapes=[pltpu.VMEM((tm, d), BF16)],
        compiler_params=_cparams(2),
        name=name,
    )(x, mod.arr, mod.arr, w, gains, *tables)


def _oproj_kernel(o_ref, w_ref, x_ref, g_ref, out_ref):
    acc = jnp.dot(o_ref[...].astype(BF16), w_ref[...].astype(BF16), preferred_element_type=F32)
    out_ref[...] = x_ref[...] + g_ref[...] * acc


def _out_proj(o, w, w_layer, x, mod, layer, gate_chunk, tm, tn, name):
    m, d = x.shape
    k = o.shape[1]
    return pl.pallas_call(
        _oproj_kernel,
        grid=(m // tm, d // tn),
        in_specs=[
            pl.BlockSpec((tm, k), lambda i, j: (i, 0)),
            pl.BlockSpec((None, k, tn), lambda i, j: (w_layer, 0, j)),
            pl.BlockSpec((tm, tn), lambda i, j: (i, j)),
            mod.spec(layer, gate_chunk, tn, tm, True),
        ],
        out_specs=pl.BlockSpec((tm, tn), lambda i, j: (i, j)),
        out_shape=jax.ShapeDtypeStruct((m, d), F32),
        compiler_params=_cparams(2),
        name=name,
    )(o, w, x, mod.arr)


def _attn_a_prompt_kernel(q_ref, kp_ref, kc_ref, vp_ref, vc_ref, o_ref, kk, vv, ob, lb):
    first = pl.program_id(2) == 0
    qb = A_QBLOCK
    kk[0:qb, :] = kp_ref[...]
    kk[qb:2 * qb, :] = kc_ref[...]
    vv[0:qb, :] = vp_ref[...]
    vv[qb:2 * qb, :] = vc_ref[...]
    scale = A_HEAD_DIM ** -0.5
    qi = lax.broadcasted_iota(jnp.int32, (BLOCK, 2 * BLOCK), 0) + BLOCK
    kj = lax.broadcasted_iota(jnp.int32, (BLOCK, 2 * BLOCK), 1)
    dist = qi - kj
    band = (dist >= 0) & (dist <= BLOCK)
    band_first = band & (kj >= jnp.where(first, BLOCK, 0))
    for b, r in enumerate(A_BRANCH_DILATIONS):
        for rho in range(r):
            for m in range(qb // (BLOCK * r)):
                q0 = rho + BLOCK * r * m
                k0 = qb - BLOCK * r + q0
                if r == 1:
                    qsl, ksl = pl.ds(q0, BLOCK), pl.ds(k0, 2 * BLOCK)
                else:
                    qsl, ksl = pl.ds(q0, BLOCK, stride=r), pl.ds(k0, 2 * BLOCK, stride=r)
                q = q_ref[qsl, :].astype(BF16)
                k = kk[ksl, :].astype(BF16)
                v = vv[ksl, :].astype(BF16)
                s = lax.dot_general(q, k, (((1,), (1,)), ((), ())), preferred_element_type=F32) * scale
                mask = band_first if m == 0 else band
                s = jnp.where(mask, s, NEG_INF)
                mx = jnp.max(s, axis=-1, keepdims=True)
                p = jnp.exp(s - mx)
                den = jnp.sum(p, axis=-1, keepdims=True)
                o = jnp.dot(p.astype(BF16), v, preferred_element_type=F32) / den
                ob[b, qsl, :] = o
                lb[b, qsl, :] = jnp.broadcast_to(mx + jnp.log(den), (BLOCK, LANES))
    lse = [lb[b] for b in range(3)]
    top = jnp.maximum(jnp.maximum(lse[0], lse[1]), lse[2])
    w = [jnp.exp(l - top) for l in lse]
    tot = w[0] + w[1] + w[2]
    o_ref[...] = ((w[0] * ob[0] + w[1] * ob[1] + w[2] * ob[2]) / tot).astype(o_ref.dtype)


def _attn_a_prompt(qkv, n_seq, seq_len):
    h = A_HEADS
    qb = A_QBLOCK
    qkv3 = qkv.reshape(n_seq, seq_len, 3 * h * A_HEAD_DIM)

    def blk(col0, prev):
        if prev:
            return pl.BlockSpec((None, qb, LANES), lambda n, hh, t: (n, jnp.maximum(t - 1, 0), col0 + hh))
        return pl.BlockSpec((None, qb, LANES), lambda n, hh, t: (n, t, col0 + hh))

    out = pl.pallas_call(
        _attn_a_prompt_kernel,
        grid=(n_seq, h, seq_len // qb),
        in_specs=[blk(0, False), blk(h, True), blk(h, False), blk(2 * h, True), blk(2 * h, False)],
        out_specs=pl.BlockSpec((None, qb, LANES), lambda n, hh, t: (n, t, hh)),
        out_shape=jax.ShapeDtypeStruct((n_seq, seq_len, h * A_HEAD_DIM), BF16),
        scratch_shapes=[
            pltpu.VMEM((2 * qb, LANES), F32), pltpu.VMEM((2 * qb, LANES), F32),
            pltpu.VMEM((3, qb, LANES), F32), pltpu.VMEM((3, qb, LANES), F32),
        ],
        compiler_params=_cparams(3),
        name="attn_a_prompt",
    )(qkv3, qkv3, qkv3, qkv3, qkv3)
    return out.reshape(n_seq * seq_len, h * A_HEAD_DIM)


def _attn_b_prompt_kernel(sink_ref, q_ref, kvp_ref, kvc_ref, o_ref, kvs, *, tq):
    first = pl.program_id(1) == 0
    kv_w = 2 * B_KV_HEADS * B_HEAD_DIM
    k_cols = B_KV_HEADS * B_HEAD_DIM
    kvs[0:BLOCK, :] = kvp_ref[...]
    kvs[BLOCK:BLOCK + tq, :] = kvc_ref[...]
    scale = B_HEAD_DIM ** -0.5
    qi = lax.broadcasted_iota(jnp.int32, (BLOCK, 2 * BLOCK), 0) + BLOCK
    kj = lax.broadcasted_iota(jnp.int32, (BLOCK, 2 * BLOCK), 1)
    dist = qi - kj
    band = (dist >= 0) & (dist <= BLOCK)
    lane_half = lax.broadcasted_iota(jnp.int32, (2 * BLOCK, LANES), 1) // B_HEAD_DIM

    def sub_block(sb, carry):
        row0 = pl.multiple_of(sb * BLOCK, BLOCK)
        mask = band & (kj >= jnp.where(jnp.logical_and(first, sb == 0), BLOCK, 0))
        for hk in range(B_KV_HEADS):
            cbk, hh = hk // 2, hk % 2
            kblk = kvs[pl.ds(row0, 2 * BLOCK), cbk * LANES:(cbk + 1) * LANES]
            vblk = kvs[pl.ds(row0, 2 * BLOCK), k_cols + cbk * LANES:k_cols + (cbk + 1) * LANES]
            k_half, v_half = [], []
            for a in range(2):
                ka = kblk if a == hh else pltpu.roll(kblk, B_HEAD_DIM, 1)
                va = vblk if a == hh else pltpu.roll(vblk, B_HEAD_DIM, 1)
                k_half.append(jnp.where(lane_half == a, ka, 0.0).astype(BF16))
                v_half.append(jnp.where(lane_half == a, va, 0.0).astype(BF16))
            for c in range(hk * (B_GROUP // 2), (hk + 1) * (B_GROUP // 2)):
                q2 = q_ref[pl.ds(row0, BLOCK), c * LANES:(c + 1) * LANES].astype(BF16)
                o_pair = jnp.zeros((BLOCK, LANES), F32)
                for a in range(2):
                    sink = sink_ref[2 * c + a]
                    s = lax.dot_general(q2, k_half[a], (((1,), (1,)), ((), ())),
                                        preferred_element_type=F32) * scale
                    s = jnp.where(mask, s, NEG_INF)
                    mx = jnp.maximum(jnp.max(s, axis=-1, keepdims=True), sink)
                    p = jnp.exp(s - mx)
                    den = jnp.sum(p, axis=-1, keepdims=True) + jnp.exp(sink - mx)
                    o_pair = o_pair + jnp.dot(p.astype(BF16), v_half[a], preferred_element_type=F32) / den
                o_ref[pl.ds(row0, BLOCK), c * LANES:(c + 1) * LANES] = o_pair.astype(o_ref.dtype)
        return carry

    lax.fori_loop(0, tq // BLOCK, sub_block, 0)


def _attn_b_prompt(q, kv, sinks, n_seq, seq_len, tq=512):
    d = q.shape[1]
    kv_w = kv.shape[1]
    q3 = q.reshape(n_seq, seq_len, d)
    kv3 = kv.reshape(n_seq, seq_len, kv_w)
    per = tq // BLOCK
    out = pl.pallas_call(
        functools.partial(_attn_b_prompt_kernel, tq=tq),
        grid_spec=pltpu.PrefetchScalarGridSpec(
            num_scalar_prefetch=0,
            grid=(n_seq, seq_len // tq),
            in_specs=[
                pl.BlockSpec(memory_space=pltpu.SMEM),
                pl.BlockSpec((None, tq, d), lambda n, t: (n, t, 0)),
                pl.BlockSpec((None, BLOCK, kv_w), lambda n, t: (n, jnp.maximum(t * per - 1, 0), 0)),
                pl.BlockSpec((None, tq, kv_w), lambda n, t: (n, t, 0)),
            ],
            out_specs=pl.BlockSpec((None, tq, d), lambda n, t: (n, t, 0)),
            scratch_shapes=[pltpu.VMEM((BLOCK + tq, kv_w), F32)],
        ),
        out_shape=jax.ShapeDtypeStruct((n_seq, seq_len, d), BF16),
        compiler_params=_cparams(2),
        name="attn_b_prompt",
    )(sinks, q3, kv3, kv3)
    return out.reshape(n_seq * seq_len, d)


def _attn_a_step_kernel(qkv_ref, k1_ref, k4_ref, k16_ref, v1_ref, v4_ref, v16_ref, o_ref):
    scale = A_HEAD_DIM ** -0.5
    n_br = len(A_BRANCH_DILATIONS)
    q = qkv_ref[0:A_HEADS, :]
    k_new = qkv_ref[A_HEADS:2 * A_HEADS, :]
    v_new = qkv_ref[2 * A_HEADS:3 * A_HEADS, :]
    s_new = jnp.sum(q * k_new, axis=-1, keepdims=True) * scale
    s = [jnp.sum(k_ref[...] * q[None], axis=-1, keepdims=True) * scale
         for k_ref in (k1_ref, k4_ref, k16_ref)]
    top = s_new
    for sb in s:
        top = jnp.maximum(top, jnp.max(sb, axis=0))
    p_new = jnp.exp(s_new - top)
    den = n_br * p_new
    acc = (n_br * p_new) * v_new
    for sb, v_ref in zip(s, (v1_ref, v4_ref, v16_ref)):
        p = jnp.exp(sb - top[None])
        den = den + jnp.sum(p, axis=0)
        acc = acc + jnp.sum(p * v_ref[...], axis=0)
    o_ref[...] = (acc / den).astype(o_ref.dtype)


def _attn_a_step(qkv, cache_k, cache_v, layer):
    n = qkv.shape[0]
    d = A_HEADS * A_HEAD_DIM
    n_layers, _, buf = cache_k.shape[:3]
    assert buf == BLOCK * A_BRANCH_DILATIONS[-1]
    qkv3 = qkv.reshape(n, 3 * A_HEADS, A_HEAD_DIM)

    views, specs = [], []
    for cache in (cache_k, cache_v):
        for r in A_BRANCH_DILATIONS:
            views.append(cache.reshape(n_layers, n, buf // r, r, A_HEADS, A_HEAD_DIM))
            last = buf // (r * BLOCK) - 1
            specs.append(pl.BlockSpec((None, None, BLOCK, None, A_HEADS, A_HEAD_DIM),
                                      lambda i, last=last: (layer, i, last, 0, 0, 0)))
    out = pl.pallas_call(
        _attn_a_step_kernel,
        grid=(n,),
        in_specs=[pl.BlockSpec((None, 3 * A_HEADS, A_HEAD_DIM), lambda i: (i, 0, 0))] + specs,
        out_specs=pl.BlockSpec((None, A_HEADS, A_HEAD_DIM), lambda i: (i, 0, 0)),
        out_shape=jax.ShapeDtypeStruct((n, A_HEADS, A_HEAD_DIM), BF16),
        compiler_params=_cparams(1),
        name="attn_a_step",
    )(qkv3, *views)
    return out.reshape(n, d)


def _attn_b_step_kernel(q_ref, kvn_ref, ck_ref, cv_ref, sink_ref, o_ref):
    scale = B_HEAD_DIM ** -0.5
    for hk in range(B_KV_HEADS):
        rows = slice(hk * B_GROUP, (hk + 1) * B_GROUP)
        qg = q_ref[rows, :]
        k = ck_ref[:, hk, :]
        v = cv_ref[:, hk, :]
        k_new = kvn_ref[hk:hk + 1, :]
        v_new = kvn_ref[B_KV_HEADS + hk:B_KV_HEADS + hk + 1, :]
        sink = sink_ref[rows, :]
        s = lax.dot_general(qg.astype(BF16), k.astype(BF16), (((1,), (1,)), ((), ())),
                            preferred_element_type=F32) * scale
        s_new = jnp.sum(qg * k_new, axis=-1, keepdims=True) * scale
        top = jnp.maximum(jnp.maximum(jnp.max(s, axis=-1, keepdims=True), s_new), sink)
        p = jnp.exp(s - top)
        p_new = jnp.exp(s_new - top)
        den = jnp.sum(p, axis=-1, keepdims=True) + p_new + jnp.exp(sink - top)
        acc = jnp.dot(p.astype(BF16), v.astype(BF16), preferred_element_type=F32) + p_new * v_new
        o_ref[rows, :] = acc / den


def _attn_b_step(q, kv_new, cache_k, cache_v, sinks):
    n = q.shape[0]
    win = cache_k.shape[1]
    assert win == BLOCK
    q3 = q.reshape(n, B_HEADS, B_HEAD_DIM)
    kvn = kv_new.reshape(n, 2 * B_KV_HEADS, B_HEAD_DIM)
    cspec = pl.BlockSpec((None, win, B_KV_HEADS, B_HEAD_DIM), lambda i: (i, 0, 0, 0))
    out = pl.pallas_call(
        _attn_b_step_kernel,
        grid=(n,),
        in_specs=[
            pl.BlockSpec((None, B_HEADS, B_HEAD_DIM), lambda i: (i, 0, 0)),
            pl.BlockSpec((None, 2 * B_KV_HEADS, B_HEAD_DIM), lambda i: (i, 0, 0)),
            cspec, cspec,
            pl.BlockSpec((B_HEADS, 1), lambda i: (0, 0)),
        ],
        out_specs=pl.BlockSpec((None, B_HEADS, B_HEAD_DIM), lambda i: (i, 0, 0)),
        out_shape=jax.ShapeDtypeStruct((n, B_HEADS, B_HEAD_DIM), F32),
        compiler_params=_cparams(1),
        name="attn_b_step",
    )(q3, kvn, cache_k, cache_v, sinks.reshape(B_HEADS, 1))
    return out.reshape(n, B_HEADS * B_HEAD_DIM)


def _shift_b_kernel(kvn_ref, ck_ref, cv_ref, nk_ref, nv_ref):
    win = ck_ref.shape[0]
    nk_ref[0:win - 1] = ck_ref[1:win]
    nv_ref[0:win - 1] = cv_ref[1:win]
    nk_ref[win - 1] = kvn_ref[0:B_KV_HEADS, :]
    nv_ref[win - 1] = kvn_ref[B_KV_HEADS:2 * B_KV_HEADS, :]


def _shift_b_cache(kv_new, cache_k, cache_v):
    n, win = cache_k.shape[:2]
    kvn = kv_new.reshape(n, 2 * B_KV_HEADS, B_HEAD_DIM)
    cspec = pl.BlockSpec((None, win, B_KV_HEADS, B_HEAD_DIM), lambda i: (i, 0, 0, 0))
    return pl.pallas_call(
        _shift_b_kernel,
        grid=(n,),
        in_specs=[pl.BlockSpec((None, 2 * B_KV_HEADS, B_HEAD_DIM), lambda i: (i, 0, 0)), cspec, cspec],
        out_specs=[cspec, cspec],
        out_shape=[jax.ShapeDtypeStruct(cache_k.shape, F32), jax.ShapeDtypeStruct(cache_v.shape, F32)],
        compiler_params=_cparams(1),
        name="shift_b_cache",
    )(kvn, cache_k, cache_v)


SHIFT_ROWS = 512


def _shift_a_kernel(kc_ref, kx_ref, kn_ref, vc_ref, vx_ref, vn_ref, nk_ref, nv_ref):
    t = pl.program_id(2)
    last = pl.num_programs(2) - 1
    rows = nk_ref.shape[0]
    for cur, nxt, new, out in ((kc_ref, kx_ref, kn_ref, nk_ref), (vc_ref, vx_ref, vn_ref, nv_ref)):
        out[0:rows - 1] = cur[1:rows]

        @pl.when(t < last)
        def _(nxt=nxt, out=out):
            out[rows - 1] = nxt[0]

        @pl.when(t == last)
        def _(new=new, out=out):
            out[rows - 1] = new[0]


def _shift_a_cache(qkv_layers, cache_k, cache_v):
    n_layers, n_seq, buf = cache_k.shape[:3]
    d = A_HEADS * A_HEAD_DIM
    rows = SHIFT_ROWS
    k_new = jnp.stack([q[:, d:2 * d] for q in qkv_layers]).reshape(n_layers, n_seq, 1, A_HEADS, A_HEAD_DIM)
    v_new = jnp.stack([q[:, 2 * d:] for q in qkv_layers]).reshape(n_layers, n_seq, 1, A_HEADS, A_HEAD_DIM)
    cur = pl.BlockSpec((None, None, rows, A_HEADS, A_HEAD_DIM), lambda l, n, t: (l, n, t, 0, 0))
    nxt = pl.BlockSpec((None, None, 1, A_HEADS, A_HEAD_DIM),
                       lambda l, n, t: (l, n, jnp.minimum((t + 1) * rows, buf - 1), 0, 0))
    new = pl.BlockSpec((None, None, 1, A_HEADS, A_HEAD_DIM), lambda l, n, t: (l, n, 0, 0, 0))
    return pl.pallas_call(
        _shift_a_kernel,
        grid=(n_layers, n_seq, buf // rows),
        in_specs=[cur, nxt, new, cur, nxt, new],
        out_specs=[cur, cur],
        out_shape=[jax.ShapeDtypeStruct(cache_k.shape, F32), jax.ShapeDtypeStruct(cache_v.shape, F32)],
        compiler_params=_cparams(3),
        name="shift_a_cache",
    )(cache_k, cache_k, k_new, cache_v, cache_v, v_new)


def _route_tile(x_ref, sh_ref, sc_ref, rw_ref, rb_ref, h_ref, e_ref, w_ref, r_ref, cnt_ref):
    h = _adaln(x_ref[...], sh_ref[...], sc_ref[...])
    h_ref[...] = h
    logits = lax.dot_general(rw_ref[...], h, (((1,), (1,)), ((), ())),
                             precision=lax.Precision.HIGHEST, preferred_element_type=F32)
    scores = jax.nn.sigmoid(logits)
    sel = scores + rb_ref[...]
    tm = sel.shape[1]
    pos = lax.broadcasted_iota(jnp.int32, (EXPERTS_PER_GROUP, tm), 0)

    def top2(v):
        m1 = jnp.max(v, axis=0, keepdims=True)
        i1 = jnp.min(jnp.where(v == m1, pos, EXPERTS_PER_GROUP), axis=0, keepdims=True)
        v2 = jnp.where(pos == i1, -jnp.inf, v)
        m2 = jnp.max(v2, axis=0, keepdims=True)
        i2 = jnp.min(jnp.where(v2 == m2, pos, EXPERTS_PER_GROUP), axis=0, keepdims=True)
        return m1 + m2, i1, i2

    best, e1, e2 = None, None, None
    for g in range(N_GROUPS):
        gs, i1, i2 = top2(sel[g * EXPERTS_PER_GROUP:(g + 1) * EXPERTS_PER_GROUP, :])
        i1 = i1 + g * EXPERTS_PER_GROUP
        i2 = i2 + g * EXPERTS_PER_GROUP
        if g == 0:
            best, e1, e2 = gs, i1, i2
        else:
            take = gs > best
            best = jnp.where(take, gs, best)
            e1 = jnp.where(take, i1, e1)
            e2 = jnp.where(take, i2, e2)
    eid = lax.broadcasted_iota(jnp.int32, (N_EXPERTS, tm), 0)
    w1 = jnp.sum(jnp.where(eid == e1, scores, 0.0), axis=0, keepdims=True)
    w2 = jnp.sum(jnp.where(eid == e2, scores, 0.0), axis=0, keepdims=True)
    tot = w1 + w2
    e_ref[0:1, :] = e1
    e_ref[1:2, :] = e2
    w_ref[0:1, :] = w1 / tot
    w_ref[1:2, :] = w2 / tot
    own1, own2 = eid == e1, eid == e2
    earlier = (lax.broadcasted_iota(jnp.int32, (tm, tm), 0)
               < lax.broadcasted_iota(jnp.int32, (tm, tm), 1)).astype(BF16)
    pre1 = jnp.dot(own1.astype(BF16), earlier, preferred_element_type=F32)
    pre2 = jnp.dot(own2.astype(BF16), earlier, preferred_element_type=F32)
    n1 = jnp.sum(own1.astype(F32), axis=1, keepdims=True)
    n2 = jnp.sum(own2.astype(F32), axis=1, keepdims=True)
    base = cnt_ref[...]
    r1 = jnp.sum(jnp.where(own1, pre1 + base, 0.0), axis=0, keepdims=True)
    r2 = jnp.sum(jnp.where(own2, pre2 + (base + n1), 0.0), axis=0, keepdims=True)
    r_ref[0:1, :] = r1.astype(jnp.int32)
    r_ref[1:2, :] = r2.astype(jnp.int32)
    cnt_ref[...] = base + n1 + n2


def _router_kernel(*refs, n_real, aliased):
    ins, outs = (refs[:7], refs[7:]) if aliased else (refs[:6], refs[6:])
    x_ref, sh_ref, sc_ref, rw_ref, rb_ref, c0_ref = ins[:6]
    h_ref, e_ref, w_ref, r_ref, cnt_ref = outs
    i = pl.program_id(0)

    @pl.when(i == 0)
    def _():
        cnt_ref[...] = c0_ref[...]

    @pl.when(i < n_real)
    def _():
        _route_tile(x_ref, sh_ref, sc_ref, rw_ref, rb_ref, h_ref, e_ref, w_ref, r_ref, cnt_ref)

    @pl.when(i >= n_real)
    def _():
        h_ref[...] = jnp.zeros_like(h_ref)


def _router(x, mod, layer, router_w_t, router_bias, counts, tm, m_total, row0, h_all, name):
    m, d = x.shape
    blk0 = row0 // tm
    n_real = m // tm
    n_steps = n_real if h_all is not None else -(-m_total // tm)
    real = lambda i: jnp.minimum(i, n_real - 1)
    in_specs = [
        pl.BlockSpec((tm, d), lambda i: (real(i), 0)),
        mod.spec(layer, 3, D_MODEL, tm, False),
        mod.spec(layer, 4, D_MODEL, tm, False),
        pl.BlockSpec((N_EXPERTS, d), lambda i: (0, 0)),
        pl.BlockSpec((N_EXPERTS, 1), lambda i: (0, 0)),
        pl.BlockSpec((N_EXPERTS, 1), lambda i: (0, 0)),
    ]
    args = [x, mod.arr, mod.arr, router_w_t, router_bias.reshape(N_EXPERTS, 1), counts]
    if h_all is not None:
        in_specs.append(pl.BlockSpec(memory_space=pl.ANY))
        args.append(h_all)
    return pl.pallas_call(
        functools.partial(_router_kernel, n_real=n_real, aliased=h_all is not None),
        grid=(n_steps,),
        in_specs=in_specs,
        out_specs=[
            pl.BlockSpec((tm, d), lambda i: (blk0 + i, 0)),
            pl.BlockSpec((2, tm), lambda i: (0, real(i))),
            pl.BlockSpec((2, tm), lambda i: (0, real(i))),
            pl.BlockSpec((2, tm), lambda i: (0, real(i))),
            pl.BlockSpec((N_EXPERTS, 1), lambda i: (0, 0)),
        ],
        out_shape=[
            jax.ShapeDtypeStruct((m_total, d), F32),
            jax.ShapeDtypeStruct((2, m), jnp.int32),
            jax.ShapeDtypeStruct((2, m), F32),
            jax.ShapeDtypeStruct((2, m), jnp.int32),
            jax.ShapeDtypeStruct((N_EXPERTS, 1), F32),
        ],
        input_output_aliases={} if h_all is None else {len(args) - 1: 0},
        compiler_params=_cparams(1),
        name=name,
    )(*args)


def _row_gather(idx_of_row, src_ref, dst_ref, sem, n_rows):
    def copy(r):
        return pltpu.make_async_copy(src_ref.at[pl.ds(idx_of_row(r), 1)], dst_ref.at[pl.ds(r, 1)], sem)

    def start():
        def body(r, carry):
            copy(r).start()
            return carry
        lax.fori_loop(0, n_rows, body, 0, unroll=8)

    def wait():
        pltpu.make_async_copy(src_ref.at[pl.ds(0, n_rows)], dst_ref, sem).wait()

    return start, wait


def _gather_kernel(e_ref, r_ref, rs_ref, cnt_ref, vt_ref, nv_ref, h_ref, xs0_ref, xs_ref, buf, sem, src, *, m_t):
    del xs0_ref
    i = pl.program_id(0)
    nv = nv_ref[0]
    cur = i % 2

    def tile(t, b):
        base = vt_ref[t] * MOE_SUB
        return _row_gather(lambda r: src[base + r], h_ref, buf.at[b], sem.at[b], MOE_SUB)

    @pl.when(i == 0)
    def _():
        def clear(s, carry):
            src[s] = jnp.int32(0)
            return carry

        for e in range(N_EXPERTS):
            used = rs_ref[e] + cnt_ref[e]
            lax.fori_loop(used, rs_ref[e] + (cnt_ref[e] + MOE_SUB - 1) // MOE_SUB * MOE_SUB, clear, 0)
        for k in range(2):
            def fill(t, carry, k=k):
                p = k * m_t + t
                src[rs_ref[e_ref[p]] + r_ref[p]] = t
                return carry

            lax.fori_loop(0, m_t, fill, 0, unroll=8)
        tile(0, 0)[0]()

    @pl.when(i < nv)
    def _():
        tile(i, cur)[1]()

        @pl.when(i + 1 < nv)
        def _():
            tile(i + 1, 1 - cur)[0]()

        xs_ref[...] = buf[cur].astype(BF16)


def _dispatch(e_flat, r_flat, region_start, counts, live_tiles, n_live, h_all, xs_init, max_live):
    m_t, d = h_all.shape
    n_slots = xs_init.shape[0]
    return pl.pallas_call(
        functools.partial(_gather_kernel, m_t=m_t),
        grid_spec=pltpu.PrefetchScalarGridSpec(
            num_scalar_prefetch=6,
            grid=(max_live,),
            in_specs=[pl.BlockSpec(memory_space=pl.ANY), pl.BlockSpec(memory_space=pl.ANY)],
            out_specs=pl.BlockSpec((MOE_SUB, d), lambda i, e, r, rs, cnt, vt, nv: (vt[jnp.minimum(i, nv[0] - 1)], 0)),
            scratch_shapes=[pltpu.VMEM((2, MOE_SUB, d), F32), pltpu.SemaphoreType.DMA((2,)),
                            pltpu.SMEM((n_slots,), jnp.int32)],
        ),
        out_shape=jax.ShapeDtypeStruct((n_slots, d), BF16),
        input_output_aliases={7: 0},
        compiler_params=_cparams(1),
        name="moe_dispatch",
    )(e_flat, r_flat, region_start, counts, live_tiles, n_live, h_all, xs_init)


def _experts_kernel(ce_ref, cs_ref, nu_ref, xs_ref, wg_ref, wu_ref, wd_ref, y_ref):
    c = pl.program_id(0)
    j = pl.program_id(1)

    @pl.when(c < nu_ref[0])
    def _():
        wg = wg_ref[...].astype(BF16)
        wu = wu_ref[...].astype(BF16)
        wd = wd_ref[...].astype(BF16)
        n_live = cs_ref[c]
        d = y_ref.shape[1]
        for n in range(1, MOE_CHUNK_SUBS + 1):
            live = n * MOE_SUB

            @pl.when(n_live == n)
            def _(live=live):
                x = xs_ref[0:live, :]
                a = jnp.dot(x, wg, preferred_element_type=F32)
                u = jnp.dot(x, wu, preferred_element_type=F32)
                hidden = (a * jax.nn.sigmoid(a) * u).astype(BF16)
                for c0 in range(0, d, MOE_DOWN_COLS):
                    cols = slice(c0, c0 + MOE_DOWN_COLS)
                    part = jnp.dot(hidden, wd[:, cols], preferred_element_type=F32)

                    @pl.when(j == 0)
                    def _(part=part, cols=cols):
                        y_ref[0:live, cols] = part

                    @pl.when(j > 0)
                    def _(part=part, cols=cols):
                        y_ref[0:live, cols] += part

                if live < y_ref.shape[0]:
                    @pl.when(j == 0)
                    def _():
                        y_ref[live:, :] = jnp.zeros((y_ref.shape[0] - live, d), F32)

    @pl.when(jnp.logical_and(c >= nu_ref[0], j == 0))
    def _():
        y_ref[...] = jnp.zeros_like(y_ref)


def _experts(chunk_expert, chunk_subs, n_used, xs, w_gate, w_up, w_down, layer):
    n_slots, d = xs.shape
    f = w_gate.shape[-1]
    tf = MOE_TF
    nf = f // tf
    ch = MOE_SUB * MOE_CHUNK_SUBS

    def row(c, nu):
        return jnp.minimum(c, nu[0] - 1)

    def col(c, j, nu):
        return jnp.where(c < nu[0], j, nf - 1)

    return pl.pallas_call(
        _experts_kernel,
        grid_spec=pltpu.PrefetchScalarGridSpec(
            num_scalar_prefetch=3,
            grid=(n_slots // ch, nf),
            in_specs=[
                pl.BlockSpec((ch, d), lambda c, j, ce, cs, nu: (row(c, nu), 0)),
                pl.BlockSpec((None, None, d, tf), lambda c, j, ce, cs, nu: (layer, ce[c], 0, col(c, j, nu))),
                pl.BlockSpec((None, None, d, tf), lambda c, j, ce, cs, nu: (layer, ce[c], 0, col(c, j, nu))),
                pl.BlockSpec((None, None, tf, d), lambda c, j, ce, cs, nu: (layer, ce[c], col(c, j, nu), 0)),
            ],
            out_specs=pl.BlockSpec((ch, d), lambda c, j, ce, cs, nu: (c, 0)),
        ),
        out_shape=jax.ShapeDtypeStruct((n_slots, d), F32),
        compiler_params=_cparams(2),
        name="moe_experts",
    )(chunk_expert, chunk_subs, n_used, xs, w_gate, w_up, w_down)


def _combine_kernel(e_ref, r_ref, rs_ref, y_ref, x_ref, w_ref, g_ref, o_ref, buf, sem, *, tm, tok0, m_t):
    i = pl.program_id(0)
    n = pl.num_programs(0)
    cur = i % 2

    def tile(t, b, k):
        base = k * m_t + tok0 + t * tm
        return _row_gather(lambda r: rs_ref[e_ref[base + r]] + r_ref[base + r], y_ref, buf.at[b, k],
                           sem.at[b], tm)

    @pl.when(i == 0)
    def _():
        tile(0, 0, 0)[0]()
        tile(0, 0, 1)[0]()

    tile(i, cur, 0)[1]()
    tile(i, cur, 1)[1]()

    @pl.when(i + 1 < n)
    def _():
        tile(i + 1, 1 - cur, 0)[0]()
        tile(i + 1, 1 - cur, 1)[0]()

    w = w_ref[...]
    moe = w[:, 0:1] * buf[cur, 0] + w[:, 1:2] * buf[cur, 1]
    o_ref[...] = x_ref[...] + g_ref[...] * moe


def _combine(e_flat, r_flat, region_start, y, x, w_t, mod, layer, tok0, m_t, tm, name):
    m, d = x.shape
    return pl.pallas_call(
        functools.partial(_combine_kernel, tm=tm, tok0=tok0, m_t=m_t),
        grid_spec=pltpu.PrefetchScalarGridSpec(
            num_scalar_prefetch=3,
            grid=(m // tm,),
            in_specs=[
                pl.BlockSpec(memory_space=pl.ANY),
                pl.BlockSpec((tm, d), lambda i, *_: (i, 0)),
                pl.BlockSpec((tm, 2), lambda i, *_: (tok0 // tm + i, 0)),
                mod.spec(layer, 5, D_MODEL, tm, False),
            ],
            out_specs=pl.BlockSpec((tm, d), lambda i, *_: (i, 0)),
            scratch_shapes=[pltpu.VMEM((2, 2, tm, d), F32), pltpu.SemaphoreType.DMA((2,))],
        ),
        out_shape=jax.ShapeDtypeStruct((m, d), F32),
        compiler_params=_cparams(1),
        name=name,
    )(e_flat, r_flat, region_start, y, x, w_t, mod.arr)


def _slot_plan(counts, max_chunks, max_live):
    sub, per = MOE_SUB, MOE_CHUNK_SUBS
    n_sub = (counts + sub - 1) // sub
    n_chunk = (n_sub + per - 1) // per
    chunk_end = jnp.cumsum(n_chunk)
    chunk_base = chunk_end - n_chunk
    n_used = chunk_end[-1]
    c = jnp.minimum(jnp.arange(max_chunks, dtype=jnp.int32), n_used - 1)
    owner = lambda ends, i: jnp.minimum(jnp.sum(i[:, None] >= ends[None, :], axis=1), N_EXPERTS - 1)
    chunk_expert = owner(chunk_end, c)
    chunk_subs = jnp.clip(n_sub[chunk_expert] - (c - chunk_base[chunk_expert]) * per, 0, per)
    sub_end = jnp.cumsum(n_sub)
    n_live = sub_end[-1]
    t = jnp.minimum(jnp.arange(max_live, dtype=jnp.int32), n_live - 1)
    t_expert = owner(sub_end, t)
    live_tiles = chunk_base[t_expert] * per + (t - (sub_end - n_sub)[t_expert])
    i32 = lambda a: a.astype(jnp.int32)
    return (i32(chunk_base * (per * sub)), i32(live_tiles), i32(n_live).reshape(1), i32(chunk_expert),
            i32(chunk_subs), i32(n_used).reshape(1))


def _moe_layer(x_p, x_s, mod_p, mod_s, layer, router_w_t, router_bias, w_gate, w_up, w_down, xs_prev):
    m_p, d = x_p.shape
    m_s = x_s.shape[0]
    m_t = m_p + m_s
    sub, per = MOE_SUB, MOE_CHUNK_SUBS
    max_live = (2 * m_t) // sub + N_EXPERTS
    max_chunks = max_live // per + N_EXPERTS
    n_slots = max_chunks * per * sub
    if xs_prev is None:
        xs_prev = jnp.zeros((n_slots, d), BF16)
    zero = jnp.zeros((N_EXPERTS, 1), F32)
    h_all, e_p, w_p, r_p, cnt = _router(x_p, mod_p, layer, router_w_t, router_bias, zero, 512, m_t, 0, None,
                                        "router_prompt")
    h_all, e_s, w_s, r_s, cnt = _router(x_s, mod_s, layer, router_w_t, router_bias, cnt, m_s, m_t, m_p, h_all,
                                        "router_sample")
    e_flat = jnp.concatenate([e_p, e_s], axis=1).reshape(-1)
    r_flat = jnp.concatenate([r_p, r_s], axis=1).reshape(-1)
    w_t = jnp.concatenate([w_p, w_s], axis=1).T
    counts = cnt[:, 0].astype(jnp.int32)
    region_start, live_tiles, n_live, chunk_expert, chunk_subs, n_used = _slot_plan(counts, max_chunks, max_live)
    xs = _dispatch(e_flat, r_flat, region_start, counts, live_tiles, n_live, h_all, xs_prev, max_live)
    y = _experts(chunk_expert, chunk_subs, n_used, xs, w_gate, w_up, w_down, layer)
    x_p = _combine(e_flat, r_flat, region_start, y, x_p, w_t, mod_p, layer, 0, m_t, 256, "combine_prompt")
    x_s = _combine(e_flat, r_flat, region_start, y, x_s, w_t, mod_s, layer, m_p, m_t, m_s, "combine_sample")
    return x_p, x_s, xs


def kernel(x_prompt, x_sample, cache_a_k, cache_a_v, cache_b_k, cache_b_v, c_prompt, c_sample, a_w_qkv, a_q_gain, a_k_gain, a_w_o, b_w_q, b_q_gain, b_sinks, b_w_o, kv_w, kv_k_gain, kv_mod_w, kv_mod_b, mod_w, mod_b, router_w, router_bias, moe_w_gate, moe_w_up, moe_w_down):
    n_seq, seq_len, d = x_prompt.shape
    n_smp = x_sample.shape[0]
    m_p = n_seq * seq_len
    x_p = x_prompt.reshape(m_p, d)
    x_s = x_sample.reshape(n_smp, d)

    n_rows = -(-(n_smp + n_seq) // 8) * 8
    c_all = jnp.concatenate([c_sample, c_prompt, jnp.zeros((n_rows - n_smp - n_seq, d), F32)], axis=0)
    mod = _modulation(c_all, mod_w, mod_b)
    kv_mod = _modulation(c_all, kv_mod_w[None], kv_mod_b[None])
    mod_p = _Mod(mod, n_smp, False, seq_len)
    mod_s = _Mod(mod, n_smp, True)
    kvmod_p = _Mod(kv_mod, n_smp, False, seq_len)
    kvmod_s = _Mod(kv_mod, n_smp, True)

    pos_p = jnp.arange(seq_len, dtype=jnp.int32)
    pos_s = jnp.full((n_smp,), PAST_LEN, dtype=jnp.int32)
    tab_a_p, tab_a_s = _rope_tables(pos_p, A_HEAD_DIM), _rope_tables(pos_s, A_HEAD_DIM)
    tab_b_p, tab_b_s = _rope_tables(pos_p, B_HEAD_DIM), _rope_tables(pos_s, B_HEAD_DIM)

    router_w_t = router_w.T
    a_gains = jnp.stack([a_q_gain, a_k_gain], axis=1).reshape(N_A_LAYERS, 2, 1, A_HEAD_DIM)
    b_gains = jnp.concatenate([b_q_gain, b_q_gain], axis=-1).reshape(-1, 1, 1, LANES)
    kv_gain = jnp.concatenate([kv_k_gain, kv_k_gain]).reshape(1, 1, LANES)
    da = A_HEADS * A_HEAD_DIM

    tm_p, tn = 1024, 512
    qkv_s_layers, ak_p, av_p = [], [], []
    kv_p = kv_s = xs_buf = None
    for layer in range(DEPTH):
        if layer < N_A_LAYERS:
            qkv_p = _adaln_matmul(x_p, mod_p, layer, (0, 1), a_w_qkv, layer, a_gains[layer], da // tn,
                                  2 * da // tn, A_HEAD_DIM, tab_a_p, tm_p, tn, "qkv_prompt")
            qkv_s = _adaln_matmul(x_s, mod_s, layer, (0, 1), a_w_qkv, layer, a_gains[layer], da // tn,
                                  2 * da // tn, A_HEAD_DIM, tab_a_s, n_smp, tn, "qkv_sample")
            o_p = _attn_a_prompt(qkv_p, n_seq, seq_len)
            o_s = _attn_a_step(qkv_s, cache_a_k, cache_a_v, layer)
            qkv_s_layers.append(qkv_s)
            keep = min(BLOCK * A_BRANCH_DILATIONS[-1], seq_len)
            qkv3 = qkv_p.reshape(n_seq, seq_len, 3 * da)
            ak_p.append(qkv3[:, seq_len - keep:, da:2 * da].reshape(n_seq, keep, A_HEADS, A_HEAD_DIM))
            av_p.append(qkv3[:, seq_len - keep:, 2 * da:].reshape(n_seq, keep, A_HEADS, A_HEAD_DIM))
            w_o, w_o_layer = a_w_o, layer
        else:
            jb = layer - N_A_LAYERS
            if layer == N_A_LAYERS:
                kvn = 2 * B_KV_HEADS * B_HEAD_DIM
                kv_p = _adaln_matmul(x_p, kvmod_p, 0, (0, 1), kv_w[None], 0, kv_gain, 1, 1, B_HEAD_DIM,
                                     tab_b_p, tm_p, kvn // 2, "kv_prompt")
                kv_s = _adaln_matmul(x_s, kvmod_s, 0, (0, 1), kv_w[None], 0, kv_gain, 1, 1, B_HEAD_DIM,
                                     tab_b_s, n_smp, kvn // 2, "kv_sample")
            q_p = _adaln_matmul(x_p, mod_p, layer, (0, 1), b_w_q, jb, b_gains[jb], d // tn, d // tn,
                                B_HEAD_DIM, tab_b_p, tm_p, tn, "q_prompt")
            q_s = _adaln_matmul(x_s, mod_s, layer, (0, 1), b_w_q, jb, b_gains[jb], d // tn, d // tn,
                                B_HEAD_DIM, tab_b_s, n_smp, tn, "q_sample")
            o_p = _attn_b_prompt(q_p, kv_p, b_sinks[jb], n_seq, seq_len)
            o_s = _attn_b_step(q_s, kv_s, cache_b_k, cache_b_v, b_sinks[jb])
            w_o, w_o_layer = b_w_o, jb
        x_p = _out_proj(o_p, w_o, w_o_layer, x_p, mod_p, layer, 2, tm_p, tn, "oproj_prompt")
        x_s = _out_proj(o_s, w_o, w_o_layer, x_s, mod_s, layer, 2, n_smp, tn, "oproj_sample")
        x_p, x_s, xs_buf = _moe_layer(x_p, x_s, mod_p, mod_s, layer, router_w_t, router_bias,
                                      moe_w_gate, moe_w_up, moe_w_down, xs_buf)

    ak_s, av_s = _shift_a_cache(qkv_s_layers, cache_a_k, cache_a_v)
    bk_s, bv_s = _shift_b_cache(kv_s, cache_b_k, cache_b_v)
    keep_b = min(BLOCK, seq_len)
    kvh = B_KV_HEADS * B_HEAD_DIM
    kv_tail = kv_p.reshape(n_seq, seq_len, 2 * kvh)[:, seq_len - keep_b:]
    bk_p = kv_tail[..., :kvh].reshape(n_seq, keep_b, B_KV_HEADS, B_HEAD_DIM)
    bv_p = kv_tail[..., kvh:].reshape(n_seq, keep_b, B_KV_HEADS, B_HEAD_DIM)
    return (x_p.reshape(n_seq, seq_len, d), x_s.reshape(n_smp, 1, d),
            jnp.stack(ak_p), jnp.stack(av_p), bk_p, bv_p, ak_s, av_s, bk_s, bv_s)
```

```python
import functools
import math

import jax
import jax.numpy as jnp
from jax import lax
from jax.experimental import pallas as pl
from jax.experimental.pallas import tpu as pltpu

F32 = jnp.float32
BF16 = jnp.bfloat16

D_MODEL = 2048
DEPTH = 4
N_A_LAYERS = DEPTH // 2
PAST_LEN = 16384
A_HEADS = 16
A_HEAD_DIM = 128
A_BRANCH_DILATIONS = (1, 4, 16)
B_HEADS = 32
B_HEAD_DIM = 64
B_KV_HEADS = 4
B_GROUP = B_HEADS // B_KV_HEADS
ROPE_THETA = 500000.0
N_EXPERTS = 16
N_GROUPS = 4
EXPERTS_PER_GROUP = N_EXPERTS // N_GROUPS
D_EXPERT = D_MODEL // 2
BLOCK = 128
LANES = 128
NORM_ROWS = 256
NORM_EPS = 1e-6
NEG_INF = -1e30

A_QBLOCK = 2048
MOE_SUB = 256
MOE_CHUNK_SUBS = 5
MOE_TF = 256
MOE_DOWN_COLS = 512
MOE_ROW_GROUP = 512
VMEM_LIMIT = 56 * 1024 * 1024


def _cparams(n_axes, vmem=VMEM_LIMIT):
    return pltpu.CompilerParams(dimension_semantics=("arbitrary",) * n_axes, vmem_limit_bytes=vmem)


def _mod_kernel(c_ref, w_ref, b_ref, o_ref):
    c = c_ref[...]
    h = (c * jax.nn.sigmoid(c)).astype(BF16)
    o_ref[...] = jnp.dot(h, w_ref[...].astype(BF16), preferred_element_type=F32) + b_ref[...]


def _modulation(c_all, w, b, tn=1024):
    n_layers, d, n = w.shape
    r = c_all.shape[0]
    return pl.pallas_call(
        _mod_kernel,
        grid=(n_layers, n // tn),
        in_specs=[
            pl.BlockSpec((r, d), lambda l, j: (0, 0)),
            pl.BlockSpec((None, d, tn), lambda l, j: (l, 0, j)),
            pl.BlockSpec((None, 1, tn), lambda l, j: (l, 0, j)),
        ],
        out_specs=pl.BlockSpec((None, r, tn), lambda l, j: (l, 0, j)),
        out_shape=jax.ShapeDtypeStruct((n_layers, r, n), F32),
        compiler_params=_cparams(2),
        name="modulation",
    )(c_all, w, b.reshape(n_layers, 1, n))


class _Mod:
    def __init__(self, mod, n_sample, per_row, rows_per_seq=None):
        self.per_row = per_row
        self.n_sample = n_sample
        self.rows_per_seq = rows_per_seq
        n_layers, r, n = mod.shape
        self.arr = mod if per_row else mod.reshape(n_layers, r, 1, n)

    def spec(self, layer, chunk, width, tm, col_from_j):
        per = D_MODEL // width

        def col(rest):
            return chunk * per + (rest[0] if col_from_j else 0)

        if self.per_row:
            return pl.BlockSpec((None, self.n_sample, width), lambda i, *rest: (layer, 0, col(rest)))
        tiles_per_seq = self.rows_per_seq // tm
        base = self.n_sample
        return pl.BlockSpec((None, None, 1, width),
                            lambda i, *rest: (layer, base + i // tiles_per_seq, 0, col(rest)))


def _rope_tables(pos, head_dim):
    rot = head_dim // 4
    half = rot // 2
    inv = jnp.exp(jnp.arange(half, dtype=F32) * (-math.log(ROPE_THETA) / half))
    ang = pos.astype(F32)[:, None] * inv[None, :]
    cos, sin = jnp.cos(ang), jnp.sin(ang)
    lane = jnp.arange(LANES) % head_dim
    idx = lane % half
    c = jnp.where(lane[None, :] < rot, cos[:, idx], 1.0)
    s1 = jnp.where(((lane >= half) & (lane < rot))[None, :], sin[:, idx], 0.0)
    s2 = jnp.where((lane < half)[None, :], -sin[:, idx], 0.0)
    return c.astype(F32), s1.astype(F32), s2.astype(F32)


def _adaln(x, shift, scale):
    r = lax.rsqrt(jnp.mean(x * x, axis=-1, keepdims=True) + NORM_EPS)
    return x * r * (1.0 + scale) + shift


def _adaln_mm_kernel(x_ref, sh_ref, sc_ref, w_ref, g_ref, c_ref, s1_ref, s2_ref, o_ref, h_ref,
                     *, head_dim, n_norm, n_tiles, tn, row_groups):
    j = pl.program_id(1)

    @pl.when(j == 0)
    def _():
        h_ref[...] = _adaln(x_ref[...], sh_ref[...], sc_ref[...]).astype(BF16)

    def plain():
        o_ref[...] = jnp.dot(h_ref[...], w_ref[...].astype(BF16), preferred_element_type=F32)

    def normed():
        w16 = w_ref[...].astype(BF16)
        rows_per = h_ref.shape[0] // row_groups
        for rg in range(row_groups):
            rows = slice(rg * rows_per, (rg + 1) * rows_per)
            acc = jnp.dot(h_ref[rows, :], w16, preferred_element_type=F32)
            normed_rows(acc, rows)

    def normed_rows(acc, rows):
        half = head_dim // 8
        gain = g_ref[...]
        c, s1, s2 = c_ref[rows, :], s1_ref[rows, :], s2_ref[rows, :]
        for cb in range(tn // LANES):
            a = acc[:, cb * LANES:(cb + 1) * LANES]
            sq = a * a
            if head_dim == LANES:
                ms = jnp.mean(sq, axis=-1, keepdims=True)
            else:
                lo = lax.broadcasted_iota(jnp.int32, sq.shape, 1) < head_dim
                s_lo = jnp.sum(jnp.where(lo, sq, 0.0), axis=-1, keepdims=True)
                s_hi = jnp.sum(jnp.where(lo, 0.0, sq), axis=-1, keepdims=True)
                ms = jnp.where(lo, s_lo, s_hi) * (1.0 / head_dim)
            a = a * lax.rsqrt(ms + NORM_EPS) * gain
            a = a * c + pltpu.roll(a, half, 1) * s1 + pltpu.roll(a, LANES - half, 1) * s2
            o_ref[rows, cb * LANES:(cb + 1) * LANES] = a

    if n_norm >= n_tiles:
        normed()
    elif n_norm == 0:
        plain()
    else:
        pl.when(j < n_norm)(normed)
        pl.when(j >= n_norm)(plain)


def _adaln_matmul(x, mod, layer, chunks, w, w_layer, gains, tiles_per_gain, n_norm, head_dim, tables,
                  tm, tn, name):
    m, d = x.shape
    n = w.shape[-1]
    n_tiles = n // tn
    table_rows = tables[0].shape[0]
    table_tiles = table_rows // tm
    n_gains = gains.shape[0]
    tab_spec = pl.BlockSpec((tm, LANES), lambda i, j: (i % table_tiles, 0))
    kern = functools.partial(_adaln_mm_kernel, head_dim=head_dim, n_norm=n_norm, n_tiles=n_tiles, tn=tn,
                             row_groups=max(1, tm // NORM_ROWS))
    return pl.pallas_call(
        kern,
        grid=(m // tm, n_tiles),
        in_specs=[
            pl.BlockSpec((tm, d), lambda i, j: (i, 0)),
            mod.spec(layer, chunks[0], D_MODEL, tm, False),
            mod.spec(layer, chunks[1], D_MODEL, tm, False),
            pl.BlockSpec((None, d, tn), lambda i, j: (w_layer, 0, j)),
            pl.BlockSpec((None, 1, LANES), lambda i, j: (jnp.minimum(j // tiles_per_gain, n_gains - 1), 0, 0)),
            tab_spec, tab_spec, tab_spec,
        ],
        out_specs=pl.BlockSpec((tm, tn), lambda i, j: (i, j)),
        out_shape=jax.ShapeDtypeStruct((m, n), F32),
        scratch_shapes=[pltpu.VMEM((tm, d), BF16)],
        compiler_params=_cparams(2),
        name=name,
    )(x, mod.arr, mod.arr, w, gains, *tables)


def _oproj_kernel(o_ref, w_ref, x_ref, g_ref, out_ref):
    acc = jnp.dot(o_ref[...].astype(BF16), w_ref[...].astype(BF16), preferred_element_type=F32)
    out_ref[...] = x_ref[...] + g_ref[...] * acc


def _out_proj(o, w, w_layer, x, mod, layer, gate_chunk, tm, tn, name):
    m, d = x.shape
    k = o.shape[1]
    return pl.pallas_call(
        _oproj_kernel,
        grid=(m // tm, d // tn),
        in_specs=[
            pl.BlockSpec((tm, k), lambda i, j: (i, 0)),
            pl.BlockSpec((None, k, tn), lambda i, j: (w_layer, 0, j)),
            pl.BlockSpec((tm, tn), lambda i, j: (i, j)),
            mod.spec(layer, gate_chunk, tn, tm, True),
        ],
        out_specs=pl.BlockSpec((tm, tn), lambda i, j: (i, j)),
        out_shape=jax.ShapeDtypeStruct((m, d), F32),
        compiler_params=_cparams(2),
        name=name,
    )(o, w, x, mod.arr)


def _attn_a_prompt_kernel(q_ref, kp_ref, kc_ref, vp_ref, vc_ref, o_ref, kk, vv, ob, lb):
    first = pl.program_id(2) == 0
    qb = A_QBLOCK
    kk[0:qb, :] = kp_ref[...]
    kk[qb:2 * qb, :] = kc_ref[...]
    vv[0:qb, :] = vp_ref[...]
    vv[qb:2 * qb, :] = vc_ref[...]
    scale = A_HEAD_DIM ** -0.5
    qi = lax.broadcasted_iota(jnp.int32, (BLOCK, 2 * BLOCK), 0) + BLOCK
    kj = lax.broadcasted_iota(jnp.int32, (BLOCK, 2 * BLOCK), 1)
    dist = qi - kj
    band = (dist >= 0) & (dist <= BLOCK)
    band_first = band & (kj >= jnp.where(first, BLOCK, 0))
    for b, r in enumerate(A_BRANCH_DILATIONS):
        for rho in range(r):
            for m in range(qb // (BLOCK * r)):
                q0 = rho + BLOCK * r * m
                k0 = qb - BLOCK * r + q0
                if r == 1:
                    qsl, ksl = pl.ds(q0, BLOCK), pl.ds(k0, 2 * BLOCK)
                else:
                    qsl, ksl = pl.ds(q0, BLOCK, stride=r), pl.ds(k0, 2 * BLOCK, stride=r)
                q = q_ref[qsl, :].astype(BF16)
                k = kk[ksl, :].astype(BF16)
                v = vv[ksl, :].astype(BF16)
                s = lax.dot_general(q, k, (((1,), (1,)), ((), ())), preferred_element_type=F32) * scale
                mask = band_first if m == 0 else band
                s = jnp.where(mask, s, NEG_INF)
                mx = jnp.max(s, axis=-1, keepdims=True)
                p = jnp.exp(s - mx)
                den = jnp.sum(p, axis=-1, keepdims=True)
                o = jnp.dot(p.astype(BF16), v, preferred_element_type=F32) / den
                ob[b, qsl, :] = o
                lb[b, qsl, :] = jnp.broadcast_to(mx + jnp.log(den), (BLOCK, LANES))
    lse = [lb[b] for b in range(3)]
    top = jnp.maximum(jnp.maximum(lse[0], lse[1]), lse[2])
    w = [jnp.exp(l - top) for l in lse]
    tot = w[0] + w[1] + w[2]
    o_ref[...] = ((w[0] * ob[0] + w[1] * ob[1] + w[2] * ob[2]) / tot).astype(o_ref.dtype)


def _attn_a_prompt(qkv, n_seq, seq_len):
    h = A_HEADS
    qb = A_QBLOCK
    qkv3 = qkv.reshape(n_seq, seq_len, 3 * h * A_HEAD_DIM)

    def blk(col0, prev):
        if prev:
            return pl.BlockSpec((None, qb, LANES), lambda n, hh, t: (n, jnp.maximum(t - 1, 0), col0 + hh))
        return pl.BlockSpec((None, qb, LANES), lambda n, hh, t: (n, t, col0 + hh))

    out = pl.pallas_call(
        _attn_a_prompt_kernel,
        grid=(n_seq, h, seq_len // qb),
        in_specs=[blk(0, False), blk(h, True), blk(h, False), blk(2 * h, True), blk(2 * h, False)],
        out_specs=pl.BlockSpec((None, qb, LANES), lambda n, hh, t: (n, t, hh)),
        out_shape=jax.ShapeDtypeStruct((n_seq, seq_len, h * A_HEAD_DIM), BF16),
        scratch_shapes=[
            pltpu.VMEM((2 * qb, LANES), F32), pltpu.VMEM((2 * qb, LANES), F32),
            pltpu.VMEM((3, qb, LANES), F32), pltpu.VMEM((3, qb, LANES), F32),
        ],
        compiler_params=_cparams(3),
        name="attn_a_prompt",
    )(qkv3, qkv3, qkv3, qkv3, qkv3)
    return out.reshape(n_seq * seq_len, h * A_HEAD_DIM)


def _attn_b_prompt_kernel(sink_ref, q_ref, kvp_ref, kvc_ref, o_ref, kvs, *, tq):
    first = pl.program_id(1) == 0
    kv_w = 2 * B_KV_HEADS * B_HEAD_DIM
    k_cols = B_KV_HEADS * B_HEAD_DIM
    kvs[0:BLOCK, :] = kvp_ref[...]
    kvs[BLOCK:BLOCK + tq, :] = kvc_ref[...]
    scale = B_HEAD_DIM ** -0.5
    qi = lax.broadcasted_iota(jnp.int32, (BLOCK, 2 * BLOCK), 0) + BLOCK
    kj = lax.broadcasted_iota(jnp.int32, (BLOCK, 2 * BLOCK), 1)
    dist = qi - kj
    band = (dist >= 0) & (dist <= BLOCK)
    lane_half = lax.broadcasted_iota(jnp.int32, (2 * BLOCK, LANES), 1) // B_HEAD_DIM

    def sub_block(sb, carry):
        row0 = pl.multiple_of(sb * BLOCK, BLOCK)
        mask = band & (kj >= jnp.where(jnp.logical_and(first, sb == 0), BLOCK, 0))
        for hk in range(B_KV_HEADS):
            cbk, hh = hk // 2, hk % 2
            kblk = kvs[pl.ds(row0, 2 * BLOCK), cbk * LANES:(cbk + 1) * LANES]
            vblk = kvs[pl.ds(row0, 2 * BLOCK), k_cols + cbk * LANES:k_cols + (cbk + 1) * LANES]
            k_half, v_half = [], []
            for a in range(2):
                ka = kblk if a == hh else pltpu.roll(kblk, B_HEAD_DIM, 1)
                va = vblk if a == hh else pltpu.roll(vblk, B_HEAD_DIM, 1)
                k_half.append(jnp.where(lane_half == a, ka, 0.0).astype(BF16))
                v_half.append(jnp.where(lane_half == a, va, 0.0).astype(BF16))
            for c in range(hk * (B_GROUP // 2), (hk + 1) * (B_GROUP // 2)):
                q2 = q_ref[pl.ds(row0, BLOCK), c * LANES:(c + 1) * LANES].astype(BF16)
                o_pair = jnp.zeros((BLOCK, LANES), F32)
                for a in range(2):
                    sink = sink_ref[2 * c + a]
                    s = lax.dot_general(q2, k_half[a], (((1,), (1,)), ((), ())),
                                        preferred_element_type=F32) * scale
                    s = jnp.where(mask, s, NEG_INF)
                    mx = jnp.maximum(jnp.max(s, axis=-1, keepdims=True), sink)
                    p = jnp.exp(s - mx)
                    den = jnp.sum(p, axis=-1, keepdims=True) + jnp.exp(sink - mx)
                    o_pair = o_pair + jnp.dot(p.astype(BF16), v_half[a], preferred_element_type=F32) / den
                o_ref[pl.ds(row0, BLOCK), c * LANES:(c + 1) * LANES] = o_pair.astype(o_ref.dtype)
        return carry

    lax.fori_loop(0, tq // BLOCK, sub_block, 0)


def _attn_b_prompt(q, kv, sinks, n_seq, seq_len, tq=512):
    d = q.shape[1]
    kv_w = kv.shape[1]
    q3 = q.reshape(n_seq, seq_len, d)
    kv3 = kv.reshape(n_seq, seq_len, kv_w)
    per = tq // BLOCK
    out = pl.pallas_call(
        functools.partial(_attn_b_prompt_kernel, tq=tq),
        grid_spec=pltpu.PrefetchScalarGridSpec(
            num_scalar_prefetch=0,
            grid=(n_seq, seq_len // tq),
            in_specs=[
                pl.BlockSpec(memory_space=pltpu.SMEM),
                pl.BlockSpec((None, tq, d), lambda n, t: (n, t, 0)),
                pl.BlockSpec((None, BLOCK, kv_w), lambda n, t: (n, jnp.maximum(t * per - 1, 0), 0)),
                pl.BlockSpec((None, tq, kv_w), lambda n, t: (n, t, 0)),
            ],
            out_specs=pl.BlockSpec((None, tq, d), lambda n, t: (n, t, 0)),
            scratch_shapes=[pltpu.VMEM((BLOCK + tq, kv_w), F32)],
        ),
        out_shape=jax.ShapeDtypeStruct((n_seq, seq_len, d), BF16),
        compiler_params=_cparams(2),
        name="attn_b_prompt",
    )(sinks, q3, kv3, kv3)
    return out.reshape(n_seq * seq_len, d)


def _attn_a_step_kernel(qkv_ref, k1_ref, k4_ref, k16_ref, v1_ref, v4_ref, v16_ref, o_ref):
    scale = A_HEAD_DIM ** -0.5
    n_br = len(A_BRANCH_DILATIONS)
    q = qkv_ref[0:A_HEADS, :]
    k_new = qkv_ref[A_HEADS:2 * A_HEADS, :]
    v_new = qkv_ref[2 * A_HEADS:3 * A_HEADS, :]
    s_new = jnp.sum(q * k_new, axis=-1, keepdims=True) * scale
    s = [jnp.sum(k_ref[...] * q[None], axis=-1, keepdims=True) * scale
         for k_ref in (k1_ref, k4_ref, k16_ref)]
    top = s_new
    for sb in s:
        top = jnp.maximum(top, jnp.max(sb, axis=0))
    p_new = jnp.exp(s_new - top)
    den = n_br * p_new
    acc = (n_br * p_new) * v_new
    for sb, v_ref in zip(s, (v1_ref, v4_ref, v16_ref)):
        p = jnp.exp(sb - top[None])
        den = den + jnp.sum(p, axis=0)
        acc = acc + jnp.sum(p * v_ref[...], axis=0)
    o_ref[...] = (acc / den).astype(o_ref.dtype)


def _attn_a_step(qkv, cache_k, cache_v, layer):
    n = qkv.shape[0]
    d = A_HEADS * A_HEAD_DIM
    n_layers, _, buf = cache_k.shape[:3]
    assert buf == BLOCK * A_BRANCH_DILATIONS[-1]
    qkv3 = qkv.reshape(n, 3 * A_HEADS, A_HEAD_DIM)

    views, specs = [], []
    for cache in (cache_k, cache_v):
        for r in A_BRANCH_DILATIONS:
            views.append(cache.reshape(n_layers, n, buf // r, r, A_HEADS, A_HEAD_DIM))
            last = buf // (r * BLOCK) - 1
            specs.append(pl.BlockSpec((None, None, BLOCK, None, A_HEADS, A_HEAD_DIM),
                                      lambda i, last=last: (layer, i, last, 0, 0, 0)))
    out = pl.pallas_call(
        _attn_a_step_kernel,
        grid=(n,),
        in_specs=[pl.BlockSpec((None, 3 * A_HEADS, A_HEAD_DIM), lambda i: (i, 0, 0))] + specs,
        out_specs=pl.BlockSpec((None, A_HEADS, A_HEAD_DIM), lambda i: (i, 0, 0)),
        out_shape=jax.ShapeDtypeStruct((n, A_HEADS, A_HEAD_DIM), BF16),
        compiler_params=_cparams(1),
        name="attn_a_step",
    )(qkv3, *views)
    return out.reshape(n, d)


def _attn_b_step_kernel(q_ref, kvn_ref, ck_ref, cv_ref, sink_ref, o_ref):
    scale = B_HEAD_DIM ** -0.5
    for hk in range(B_KV_HEADS):
        rows = slice(hk * B_GROUP, (hk + 1) * B_GROUP)
        qg = q_ref[rows, :]
        k = ck_ref[:, hk, :]
        v = cv_ref[:, hk, :]
        k_new = kvn_ref[hk:hk + 1, :]
        v_new = kvn_ref[B_KV_HEADS + hk:B_KV_HEADS + hk + 1, :]
        sink = sink_ref[rows, :]
        s = lax.dot_general(qg.astype(BF16), k.astype(BF16), (((1,), (1,)), ((), ())),
                            preferred_element_type=F32) * scale
        s_new = jnp.sum(qg * k_new, axis=-1, keepdims=True) * scale
        top = jnp.maximum(jnp.maximum(jnp.max(s, axis=-1, keepdims=True), s_new), sink)
        p = jnp.exp(s - top)
        p_new = jnp.exp(s_new - top)
        den = jnp.sum(p, axis=-1, keepdims=True) + p_new + jnp.exp(sink - top)
        acc = jnp.dot(p.astype(BF16), v.astype(BF16), preferred_element_type=F32) + p_new * v_new
        o_ref[rows, :] = acc / den


def _attn_b_step(q, kv_new, cache_k, cache_v, sinks):
    n = q.shape[0]
    win = cache_k.shape[1]
    assert win == BLOCK
    q3 = q.reshape(n, B_HEADS, B_HEAD_DIM)
    kvn = kv_new.reshape(n, 2 * B_KV_HEADS, B_HEAD_DIM)
    cspec = pl.BlockSpec((None, win, B_KV_HEADS, B_HEAD_DIM), lambda i: (i, 0, 0, 0))
    out = pl.pallas_call(
        _attn_b_step_kernel,
        grid=(n,),
        in_specs=[
            pl.BlockSpec((None, B_HEADS, B_HEAD_DIM), lambda i: (i, 0, 0)),
            pl.BlockSpec((None, 2 * B_KV_HEADS, B_HEAD_DIM), lambda i: (i, 0, 0)),
            cspec, cspec,
            pl.BlockSpec((B_HEADS, 1), lambda i: (0, 0)),
        ],
        out_specs=pl.BlockSpec((None, B_HEADS, B_HEAD_DIM), lambda i: (i, 0, 0)),
        out_shape=jax.ShapeDtypeStruct((n, B_HEADS, B_HEAD_DIM), F32),
        compiler_params=_cparams(1),
        name="attn_b_step",
    )(q3, kvn, cache_k, cache_v, sinks.reshape(B_HEADS, 1))
    return out.reshape(n, B_HEADS * B_HEAD_DIM)


def _shift_b_kernel(kvn_ref, ck_ref, cv_ref, nk_ref, nv_ref):
    win = ck_ref.shape[0]
    nk_ref[0:win - 1] = ck_ref[1:win]
    nv_ref[0:win - 1] = cv_ref[1:win]
    nk_ref[win - 1] = kvn_ref[0:B_KV_HEADS, :]
    nv_ref[win - 1] = kvn_ref[B_KV_HEADS:2 * B_KV_HEADS, :]


def _shift_b_cache(kv_new, cache_k, cache_v):
    n, win = cache_k.shape[:2]
    kvn = kv_new.reshape(n, 2 * B_KV_HEADS, B_HEAD_DIM)
    cspec = pl.BlockSpec((None, win, B_KV_HEADS, B_HEAD_DIM), lambda i: (i, 0, 0, 0))
    return pl.pallas_call(
        _shift_b_kernel,
        grid=(n,),
        in_specs=[pl.BlockSpec((None, 2 * B_KV_HEADS, B_HEAD_DIM), lambda i: (i, 0, 0)), cspec, cspec],
        out_specs=[cspec, cspec],
        out_shape=[jax.ShapeDtypeStruct(cache_k.shape, F32), jax.ShapeDtypeStruct(cache_v.shape, F32)],
        compiler_params=_cparams(1),
        name="shift_b_cache",
    )(kvn, cache_k, cache_v)


SHIFT_ROWS = 512


def _shift_a_kernel(kc_ref, kx_ref, kn_ref, vc_ref, vx_ref, vn_ref, nk_ref, nv_ref):
    t = pl.program_id(2)
    last = pl.num_programs(2) - 1
    rows = nk_ref.shape[0]
    for cur, nxt, new, out in ((kc_ref, kx_ref, kn_ref, nk_ref), (vc_ref, vx_ref, vn_ref, nv_ref)):
        out[0:rows - 1] = cur[1:rows]

        @pl.when(t < last)
        def _(nxt=nxt, out=out):
            out[rows - 1] = nxt[0]

        @pl.when(t == last)
        def _(new=new, out=out):
            out[rows - 1] = new[0]


def _shift_a_cache(qkv_layers, cache_k, cache_v):
    n_layers, n_seq, buf = cache_k.shape[:3]
    d = A_HEADS * A_HEAD_DIM
    rows = SHIFT_ROWS
    k_new = jnp.stack([q[:, d:2 * d] for q in qkv_layers]).reshape(n_layers, n_seq, 1, A_HEADS, A_HEAD_DIM)
    v_new = jnp.stack([q[:, 2 * d:] for q in qkv_layers]).reshape(n_layers, n_seq, 1, A_HEADS, A_HEAD_DIM)
    cur = pl.BlockSpec((None, None, rows, A_HEADS, A_HEAD_DIM), lambda l, n, t: (l, n, t, 0, 0))
    nxt = pl.BlockSpec((None, None, 1, A_HEADS, A_HEAD_DIM),
                       lambda l, n, t: (l, n, jnp.minimum((t + 1) * rows, buf - 1), 0, 0))
    new = pl.BlockSpec((None, None, 1, A_HEADS, A_HEAD_DIM), lambda l, n, t: (l, n, 0, 0, 0))
    return pl.pallas_call(
        _shift_a_kernel,
        grid=(n_layers, n_seq, buf // rows),
        in_specs=[cur, nxt, new, cur, nxt, new],
        out_specs=[cur, cur],
        out_shape=[jax.ShapeDtypeStruct(cache_k.shape, F32), jax.ShapeDtypeStruct(cache_v.shape, F32)],
        compiler_params=_cparams(3),
        name="shift_a_cache",
    )(cache_k, cache_k, k_new, cache_v, cache_v, v_new)


def _route_tile(x_ref, sh_ref, sc_ref, rw_ref, rb_ref, h_ref, e_ref, w_ref, r_ref, cnt_ref):
    h = _adaln(x_ref[...], sh_ref[...], sc_ref[...])
    h_ref[...] = h
    logits = lax.dot_general(rw_ref[...], h, (((1,), (1,)), ((), ())),
                             precision=lax.Precision.HIGHEST, preferred_element_type=F32)
    scores = jax.nn.sigmoid(logits)
    sel = scores + rb_ref[...]
    tm = sel.shape[1]
    pos = lax.broadcasted_iota(jnp.int32, (EXPERTS_PER_GROUP, tm), 0)

    def top2(v):
        m1 = jnp.max(v, axis=0, keepdims=True)
        i1 = jnp.min(jnp.where(v == m1, pos, EXPERTS_PER_GROUP), axis=0, keepdims=True)
        v2 = jnp.where(pos == i1, -jnp.inf, v)
        m2 = jnp.max(v2, axis=0, keepdims=True)
        i2 = jnp.min(jnp.where(v2 == m2, pos, EXPERTS_PER_GROUP), axis=0, keepdims=True)
        return m1 + m2, i1, i2

    best, e1, e2 = None, None, None
    for g in range(N_GROUPS):
        gs, i1, i2 = top2(sel[g * EXPERTS_PER_GROUP:(g + 1) * EXPERTS_PER_GROUP, :])
        i1 = i1 + g * EXPERTS_PER_GROUP
        i2 = i2 + g * EXPERTS_PER_GROUP
        if g == 0:
            best, e1, e2 = gs, i1, i2
        else:
            take = gs > best
            best = jnp.where(take, gs, best)
            e1 = jnp.where(take, i1, e1)
            e2 = jnp.where(take, i2, e2)
    eid = lax.broadcasted_iota(jnp.int32, (N_EXPERTS, tm), 0)
    w1 = jnp.sum(jnp.where(eid == e1, scores, 0.0), axis=0, keepdims=True)
    w2 = jnp.sum(jnp.where(eid == e2, scores, 0.0), axis=0, keepdims=True)
    tot = w1 + w2
    e_ref[0:1, :] = e1
    e_ref[1:2, :] = e2
    w_ref[0:1, :] = w1 / tot
    w_ref[1:2, :] = w2 / tot
    own1, own2 = eid == e1, eid == e2
    earlier = (lax.broadcasted_iota(jnp.int32, (tm, tm), 0)
               < lax.broadcasted_iota(jnp.int32, (tm, tm), 1)).astype(BF16)
    pre1 = jnp.dot(own1.astype(BF16), earlier, preferred_element_type=F32)
    pre2 = jnp.dot(own2.astype(BF16), earlier, preferred_element_type=F32)
    n1 = jnp.sum(own1.astype(F32), axis=1, keepdims=True)
    n2 = jnp.sum(own2.astype(F32), axis=1, keepdims=True)
    base = cnt_ref[...]
    r1 = jnp.sum(jnp.where(own1, pre1 + base, 0.0), axis=0, keepdims=True)
    r2 = jnp.sum(jnp.where(own2, pre2 + (base + n1), 0.0), axis=0, keepdims=True)
    r_ref[0:1, :] = r1.astype(jnp.int32)
    r_ref[1:2, :] = r2.astype(jnp.int32)
    cnt_ref[...] = base + n1 + n2


def _router_kernel(*refs, n_real, aliased):
    ins, outs = (refs[:7], refs[7:]) if aliased else (refs[:6], refs[6:])
    x_ref, sh_ref, sc_ref, rw_ref, rb_ref, c0_ref = ins[:6]
    h_ref, e_ref, w_ref, r_ref, cnt_ref = outs
    i = pl.program_id(0)

    @pl.when(i == 0)
    def _():
        cnt_ref[...] = c0_ref[...]

    @pl.when(i < n_real)
    def _():
        _route_tile(x_ref, sh_ref, sc_ref, rw_ref, rb_ref, h_ref, e_ref, w_ref, r_ref, cnt_ref)

    @pl.when(i >= n_real)
    def _():
        h_ref[...] = jnp.zeros_like(h_ref)


def _router(x, mod, layer, router_w_t, router_bias, counts, tm, m_total, row0, h_all, name):
    m, d = x.shape
    blk0 = row0 // tm
    n_real = m // tm
    n_steps = n_real if h_all is not None else -(-m_total // tm)
    real = lambda i: jnp.minimum(i, n_real - 1)
    in_specs = [
        pl.BlockSpec((tm, d), lambda i: (real(i), 0)),
        mod.spec(layer, 3, D_MODEL, tm, False),
        mod.spec(layer, 4, D_MODEL, tm, False),
        pl.BlockSpec((N_EXPERTS, d), lambda i: (0, 0)),
        pl.BlockSpec((N_EXPERTS, 1), lambda i: (0, 0)),
        pl.BlockSpec((N_EXPERTS, 1), lambda i: (0, 0)),
    ]
    args = [x, mod.arr, mod.arr, router_w_t, router_bias.reshape(N_EXPERTS, 1), counts]
    if h_all is not None:
        in_specs.append(pl.BlockSpec(memory_space=pl.ANY))
        args.append(h_all)
    return pl.pallas_call(
        functools.partial(_router_kernel, n_real=n_real, aliased=h_all is not None),
        grid=(n_steps,),
        in_specs=in_specs,
        out_specs=[
            pl.BlockSpec((tm, d), lambda i: (blk0 + i, 0)),
            pl.BlockSpec((2, tm), lambda i: (0, real(i))),
            pl.BlockSpec((2, tm), lambda i: (0, real(i))),
            pl.BlockSpec((2, tm), lambda i: (0, real(i))),
            pl.BlockSpec((N_EXPERTS, 1), lambda i: (0, 0)),
        ],
        out_shape=[
            jax.ShapeDtypeStruct((m_total, d), F32),
            jax.ShapeDtypeStruct((2, m), jnp.int32),
            jax.ShapeDtypeStruct((2, m), F32),
            jax.ShapeDtypeStruct((2, m), jnp.int32),
            jax.ShapeDtypeStruct((N_EXPERTS, 1), F32),
        ],
        input_output_aliases={} if h_all is None else {len(args) - 1: 0},
        compiler_params=_cparams(1),
        name=name,
    )(*args)


def _row_gather(idx_of_row, src_ref, dst_ref, sem, n_rows):
    def copy(r):
        return pltpu.make_async_copy(src_ref.at[pl.ds(idx_of_row(r), 1)], dst_ref.at[pl.ds(r, 1)], sem)

    def start():
        def body(r, carry):
            copy(r).start()
            return carry
        lax.fori_loop(0, n_rows, body, 0, unroll=8)

    def wait():
        pltpu.make_async_copy(src_ref.at[pl.ds(0, n_rows)], dst_ref, sem).wait()

    return start, wait


def _gather_kernel(e_ref, r_ref, rs_ref, cnt_ref, vt_ref, nv_ref, h_ref, xs0_ref, xs_ref, buf, sem, src, *, m_t):
    del xs0_ref
    i = pl.program_id(0)
    nv = nv_ref[0]
    cur = i % 2

    def tile(t, b):
        base = vt_ref[t] * MOE_SUB
        return _row_gather(lambda r: src[base + r], h_ref, buf.at[b], sem.at[b], MOE_SUB)

    @pl.when(i == 0)
    def _():
        def clear(s, carry):
            src[s] = jnp.int32(0)
            return carry

        for e in range(N_EXPERTS):
            used = rs_ref[e] + cnt_ref[e]
            lax.fori_loop(used, rs_ref[e] + (cnt_ref[e] + MOE_SUB - 1) // MOE_SUB * MOE_SUB, clear, 0)
        for k in range(2):
            def fill(t, carry, k=k):
                p = k * m_t + t
                src[rs_ref[e_ref[p]] + r_ref[p]] = t
                return carry

            lax.fori_loop(0, m_t, fill, 0, unroll=8)
        tile(0, 0)[0]()

    @pl.when(i < nv)
    def _():
        tile(i, cur)[1]()

        @pl.when(i + 1 < nv)
        def _():
            tile(i + 1, 1 - cur)[0]()

        xs_ref[...] = buf[cur].astype(BF16)


def _dispatch(e_flat, r_flat, region_start, counts, live_tiles, n_live, h_all, xs_init, max_live):
    m_t, d = h_all.shape
    n_slots = xs_init.shape[0]
    return pl.pallas_call(
        functools.partial(_gather_kernel, m_t=m_t),
        grid_spec=pltpu.PrefetchScalarGridSpec(
            num_scalar_prefetch=6,
            grid=(max_live,),
            in_specs=[pl.BlockSpec(memory_space=pl.ANY), pl.BlockSpec(memory_space=pl.ANY)],
            out_specs=pl.BlockSpec((MOE_SUB, d), lambda i, e, r, rs, cnt, vt, nv: (vt[jnp.minimum(i, nv[0] - 1)], 0)),
            scratch_shapes=[pltpu.VMEM((2, MOE_SUB, d), F32), pltpu.SemaphoreType.DMA((2,)),
                            pltpu.SMEM((n_slots,), jnp.int32)],
        ),
        out_shape=jax.ShapeDtypeStruct((n_slots, d), BF16),
        input_output_aliases={7: 0},
        compiler_params=_cparams(1),
        name="moe_dispatch",
    )(e_flat, r_flat, region_start, counts, live_tiles, n_live, h_all, xs_init)


def _experts_kernel(ce_ref, cs_ref, nu_ref, xs_ref, wg_ref, wu_ref, wd_ref, y_ref):
    c = pl.program_id(0)
    j = pl.program_id(1)

    @pl.when(c < nu_ref[0])
    def _():
        wg = wg_ref[...].astype(BF16)
        wu = wu_ref[...].astype(BF16)
        wd = wd_ref[...].astype(BF16)
        n_live = cs_ref[c]
        d = y_ref.shape[1]
        for n in range(1, MOE_CHUNK_SUBS + 1):
            for first in (True, False):
                @pl.when(jnp.logical_and(n_live == n, (j == 0) == first))
                def _(n=n, first=first):
                    for r0 in range(0, n * MOE_SUB, MOE_ROW_GROUP):
                        rows = slice(r0, min(r0 + MOE_ROW_GROUP, n * MOE_SUB))
                        x = xs_ref[rows, :]
                        a = jnp.dot(x, wg, preferred_element_type=F32)
                        u = jnp.dot(x, wu, preferred_element_type=F32)
                        hidden = (a * jax.nn.sigmoid(a) * u).astype(BF16)
                        for c0 in range(0, d, MOE_DOWN_COLS):
                            cols = slice(c0, c0 + MOE_DOWN_COLS)
                            part = jnp.dot(hidden, wd[:, cols], preferred_element_type=F32)
                            if first:
                                y_ref[rows, cols] = part
                            else:
                                y_ref[rows, cols] += part
                    if first and n * MOE_SUB < y_ref.shape[0]:
                        y_ref[n * MOE_SUB:, :] = jnp.zeros((y_ref.shape[0] - n * MOE_SUB, d), F32)

    @pl.when(jnp.logical_and(c >= nu_ref[0], j == 0))
    def _():
        y_ref[...] = jnp.zeros_like(y_ref)


def _experts(chunk_expert, chunk_subs, n_used, xs, w_gate, w_up, w_down, layer):
    n_slots, d = xs.shape
    f = w_gate.shape[-1]
    tf = MOE_TF
    nf = f // tf
    ch = MOE_SUB * MOE_CHUNK_SUBS

    def row(c, nu):
        return jnp.minimum(c, nu[0] - 1)

    def col(c, j, nu):
        return jnp.where(c < nu[0], j, nf - 1)

    return pl.pallas_call(
        _experts_kernel,
        grid_spec=pltpu.PrefetchScalarGridSpec(
            num_scalar_prefetch=3,
            grid=(n_slots // ch, nf),
            in_specs=[
                pl.BlockSpec((ch, d), lambda c, j, ce, cs, nu: (row(c, nu), 0)),
                pl.BlockSpec((None, None, d, tf), lambda c, j, ce, cs, nu: (layer, ce[c], 0, col(c, j, nu))),
                pl.BlockSpec((None, None, d, tf), lambda c, j, ce, cs, nu: (layer, ce[c], 0, col(c, j, nu))),
                pl.BlockSpec((None, None, tf, d), lambda c, j, ce, cs, nu: (layer, ce[c], col(c, j, nu), 0)),
            ],
            out_specs=pl.BlockSpec((ch, d), lambda c, j, ce, cs, nu: (c, 0)),
        ),
        out_shape=jax.ShapeDtypeStruct((n_slots, d), F32),
        compiler_params=_cparams(2),
        name="moe_experts",
    )(chunk_expert, chunk_subs, n_used, xs, w_gate, w_up, w_down)


def _combine_kernel(e_ref, r_ref, rs_ref, y_ref, x_ref, w_ref, g_ref, o_ref, buf, sem, *, tm, tok0, m_t):
    i = pl.program_id(0)
    n = pl.num_programs(0)
    cur = i % 2

    def tile(t, b, k):
        base = k * m_t + tok0 + t * tm
        return _row_gather(lambda r: rs_ref[e_ref[base + r]] + r_ref[base + r], y_ref, buf.at[b, k],
                           sem.at[b], tm)

    @pl.when(i == 0)
    def _():
        tile(0, 0, 0)[0]()
        tile(0, 0, 1)[0]()

    tile(i, cur, 0)[1]()
    tile(i, cur, 1)[1]()

    @pl.when(i + 1 < n)
    def _():
        tile(i + 1, 1 - cur, 0)[0]()
        tile(i + 1, 1 - cur, 1)[0]()

    w = w_ref[...]
    moe = w[:, 0:1] * buf[cur, 0] + w[:, 1:2] * buf[cur, 1]
    o_ref[...] = x_ref[...] + g_ref[...] * moe


def _combine(e_flat, r_flat, region_start, y, x, w_t, mod, layer, tok0, m_t, tm, name):
    m, d = x.shape
    return pl.pallas_call(
        functools.partial(_combine_kernel, tm=tm, tok0=tok0, m_t=m_t),
        grid_spec=pltpu.PrefetchScalarGridSpec(
            num_scalar_prefetch=3,
            grid=(m // tm,),
            in_specs=[
                pl.BlockSpec(memory_space=pl.ANY),
                pl.BlockSpec((tm, d), lambda i, *_: (i, 0)),
                pl.BlockSpec((tm, 2), lambda i, *_: (tok0 // tm + i, 0)),
                mod.spec(layer, 5, D_MODEL, tm, False),
            ],
            out_specs=pl.BlockSpec((tm, d), lambda i, *_: (i, 0)),
            scratch_shapes=[pltpu.VMEM((2, 2, tm, d), F32), pltpu.SemaphoreType.DMA((2,))],
        ),
        out_shape=jax.ShapeDtypeStruct((m, d), F32),
        compiler_params=_cparams(1),
        name=name,
    )(e_flat, r_flat, region_start, y, x, w_t, mod.arr)


def _slot_plan(counts, max_chunks, max_live):
    sub, per = MOE_SUB, MOE_CHUNK_SUBS
    n_sub = (counts + sub - 1) // sub
    n_chunk = (n_sub + per - 1) // per
    chunk_end = jnp.cumsum(n_chunk)
    chunk_base = chunk_end - n_chunk
    n_used = chunk_end[-1]
    c = jnp.minimum(jnp.arange(max_chunks, dtype=jnp.int32), n_used - 1)
    owner = lambda ends, i: jnp.minimum(jnp.sum(i[:, None] >= ends[None, :], axis=1), N_EXPERTS - 1)
    chunk_expert = owner(chunk_end, c)
    chunk_subs = jnp.clip(n_sub[chunk_expert] - (c - chunk_base[chunk_expert]) * per, 0, per)
    sub_end = jnp.cumsum(n_sub)
    n_live = sub_end[-1]
    t = jnp.minimum(jnp.arange(max_live, dtype=jnp.int32), n_live - 1)
    t_expert = owner(sub_end, t)
    live_tiles = chunk_base[t_expert] * per + (t - (sub_end - n_sub)[t_expert])
    i32 = lambda a: a.astype(jnp.int32)
    return (i32(chunk_base * (per * sub)), i32(live_tiles), i32(n_live).reshape(1), i32(chunk_expert),
            i32(chunk_subs), i32(n_used).reshape(1))


def _moe_layer(x_p, x_s, mod_p, mod_s, layer, router_w_t, router_bias, w_gate, w_up, w_down, xs_prev):
    m_p, d = x_p.shape
    m_s = x_s.shape[0]
    m_t = m_p + m_s
    sub, per = MOE_SUB, MOE_CHUNK_SUBS
    max_live = (2 * m_t) // sub + N_EXPERTS
    max_chunks = max_live // per + N_EXPERTS
    n_slots = max_chunks * per * sub
    if xs_prev is None:
        xs_prev = jnp.zeros((n_slots, d), BF16)
    zero = jnp.zeros((N_EXPERTS, 1), F32)
    h_all, e_p, w_p, r_p, cnt = _router(x_p, mod_p, layer, router_w_t, router_bias, zero, 512, m_t, 0, None,
                                        "router_prompt")
    h_all, e_s, w_s, r_s, cnt = _router(x_s, mod_s, layer, router_w_t, router_bias, cnt, m_s, m_t, m_p, h_all,
                                        "router_sample")
    e_flat = jnp.concatenate([e_p, e_s], axis=1).reshape(-1)
    r_flat = jnp.concatenate([r_p, r_s], axis=1).reshape(-1)
    w_t = jnp.concatenate([w_p, w_s], axis=1).T
    counts = cnt[:, 0].astype(jnp.int32)
    region_start, live_tiles, n_live, chunk_expert, chunk_subs, n_used = _slot_plan(counts, max_chunks, max_live)
    xs = _dispatch(e_flat, r_flat, region_start, counts, live_tiles, n_live, h_all, xs_prev, max_live)
    y = _experts(chunk_expert, chunk_subs, n_used, xs, w_gate, w_up, w_down, layer)
    x_p = _combine(e_flat, r_flat, region_start, y, x_p, w_t, mod_p, layer, 0, m_t, 256, "combine_prompt")
    x_s = _combine(e_flat, r_flat, region_start, y, x_s, w_t, mod_s, layer, m_p, m_t, m_s, "combine_sample")
    return x_p, x_s, xs


def kernel(x_prompt, x_sample, cache_a_k, cache_a_v, cache_b_k, cache_b_v, c_prompt, c_sample, a_w_qkv, a_q_gain, a_k_gain, a_w_o, b_w_q, b_q_gain, b_sinks, b_w_o, kv_w, kv_k_gain, kv_mod_w, kv_mod_b, mod_w, mod_b, router_w, router_bias, moe_w_gate, moe_w_up, moe_w_down):
    n_seq, seq_len, d = x_prompt.shape
    n_smp = x_sample.shape[0]
    m_p = n_seq * seq_len
    x_p = x_prompt.reshape(m_p, d)
    x_s = x_sample.reshape(n_smp, d)

    n_rows = -(-(n_smp + n_seq) // 8) * 8
    c_all = jnp.concatenate([c_sample, c_prompt, jnp.zeros((n_rows - n_smp - n_seq, d), F32)], axis=0)
    mod = _modulation(c_all, mod_w, mod_b)
    kv_mod = _modulation(c_all, kv_mod_w[None], kv_mod_b[None])
    mod_p = _Mod(mod, n_smp, False, seq_len)
    mod_s = _Mod(mod, n_smp, True)
    kvmod_p = _Mod(kv_mod, n_smp, False, seq_len)
    kvmod_s = _Mod(kv_mod, n_smp, True)

    pos_p = jnp.arange(seq_len, dtype=jnp.int32)
    pos_s = jnp.full((n_smp,), PAST_LEN, dtype=jnp.int32)
    tab_a_p, tab_a_s = _rope_tables(pos_p, A_HEAD_DIM), _rope_tables(pos_s, A_HEAD_DIM)
    tab_b_p, tab_b_s = _rope_tables(pos_p, B_HEAD_DIM), _rope_tables(pos_s, B_HEAD_DIM)

    router_w_t = router_w.T
    a_gains = jnp.stack([a_q_gain, a_k_gain], axis=1).reshape(N_A_LAYERS, 2, 1, A_HEAD_DIM)
    b_gains = jnp.concatenate([b_q_gain, b_q_gain], axis=-1).reshape(-1, 1, 1, LANES)
    kv_gain = jnp.concatenate([kv_k_gain, kv_k_gain]).reshape(1, 1, LANES)
    da = A_HEADS * A_HEAD_DIM

    tm_p, tn = 1024, 512
    qkv_s_layers, ak_p, av_p = [], [], []
    kv_p = kv_s = xs_buf = None
    for layer in range(DEPTH):
        if layer < N_A_LAYERS:
            qkv_p = _adaln_matmul(x_p, mod_p, layer, (0, 1), a_w_qkv, layer, a_gains[layer], da // tn,
                                  2 * da // tn, A_HEAD_DIM, tab_a_p, tm_p, tn, "qkv_prompt")
            qkv_s = _adaln_matmul(x_s, mod_s, layer, (0, 1), a_w_qkv, layer, a_gains[layer], da // tn,
                                  2 * da // tn, A_HEAD_DIM, tab_a_s, n_smp, tn, "qkv_sample")
            o_p = _attn_a_prompt(qkv_p, n_seq, seq_len)
            o_s = _attn_a_step(qkv_s, cache_a_k, cache_a_v, layer)
            qkv_s_layers.append(qkv_s)
            keep = min(BLOCK * A_BRANCH_DILATIONS[-1], seq_len)
            qkv3 = qkv_p.reshape(n_seq, seq_len, 3 * da)
            ak_p.append(qkv3[:, seq_len - keep:, da:2 * da].reshape(n_seq, keep, A_HEADS, A_HEAD_DIM))
            av_p.append(qkv3[:, seq_len - keep:, 2 * da:].reshape(n_seq, keep, A_HEADS, A_HEAD_DIM))
            w_o, w_o_layer = a_w_o, layer
        else:
            jb = layer - N_A_LAYERS
            if layer == N_A_LAYERS:
                kvn = 2 * B_KV_HEADS * B_HEAD_DIM
                kv_p = _adaln_matmul(x_p, kvmod_p, 0, (0, 1), kv_w[None], 0, kv_gain, 1, 1, B_HEAD_DIM,
                                     tab_b_p, tm_p, kvn // 2, "kv_prompt")
                kv_s = _adaln_matmul(x_s, kvmod_s, 0, (0, 1), kv_w[None], 0, kv_gain, 1, 1, B_HEAD_DIM,
                                     tab_b_s, n_smp, kvn // 2, "kv_sample")
            q_p = _adaln_matmul(x_p, mod_p, layer, (0, 1), b_w_q, jb, b_gains[jb], d // tn, d // tn,
                                B_HEAD_DIM, tab_b_p, tm_p, tn, "q_prompt")
            q_s = _adaln_matmul(x_s, mod_s, layer, (0, 1), b_w_q, jb, b_gains[jb], d // tn, d // tn,
                                B_HEAD_DIM, tab_b_s, n_smp, tn, "q_sample")
            o_p = _attn_b_prompt(q_p, kv_p, b_sinks[jb], n_seq, seq_len)
            o_s = _attn_b_step(q_s, kv_s, cache_b_k, cache_b_v, b_sinks[jb])
            w_o, w_o_layer = b_w_o, jb
        x_p = _out_proj(o_p, w_o, w_o_layer, x_p, mod_p, layer, 2, tm_p, tn, "oproj_prompt")
        x_s = _out_proj(o_s, w_o, w_o_layer, x_s, mod_s, layer, 2, n_smp, tn, "oproj_sample")
        x_p, x_s, xs_buf = _moe_layer(x_p, x_s, mod_p, mod_s, layer, router_w_t, router_bias,
                                      moe_w_gate, moe_w_up, moe_w_down, xs_buf)

    ak_s, av_s = _shift_a_cache(qkv_s_layers, cache_a_k, cache_a_v)
    bk_s, bv_s = _shift_b_cache(kv_s, cache_b_k, cache_b_v)
    keep_b = min(BLOCK, seq_len)
    kvh = B_KV_HEADS * B_HEAD_DIM
    kv_tail = kv_p.reshape(n_seq, seq_len, 2 * kvh)[:, seq_len - keep_b:]
    bk_p = kv_tail[..., :kvh].reshape(n_seq, keep_b, B_KV_HEADS, B_HEAD_DIM)
    bv_p = kv_tail[..., kvh:].reshape(n_seq, keep_b, B_KV_HEADS, B_HEAD_DIM)
    return (x_p.reshape(n_seq, seq_len, d), x_s.reshape(n_smp, 1, d),
            jnp.stack(ak_p), jnp.stack(av_p), bk_p, bv_p, ak_s, av_s, bk_s, bv_s)
```

```python
import functools
import math

import jax
import jax.numpy as jnp
from jax import lax
from jax.experimental import pallas as pl
from jax.experimental.pallas import tpu as pltpu

F32 = jnp.float32
BF16 = jnp.bfloat16

D_MODEL = 2048
DEPTH = 4
N_A_LAYERS = DEPTH // 2
PAST_LEN = 16384
A_HEADS = 16
A_HEAD_DIM = 128
A_BRANCH_DILATIONS = (1, 4, 16)
B_HEADS = 32
B_HEAD_DIM = 64
B_KV_HEADS = 4
B_GROUP = B_HEADS // B_KV_HEADS
ROPE_THETA = 500000.0
N_EXPERTS = 16
N_GROUPS = 4
EXPERTS_PER_GROUP = N_EXPERTS // N_GROUPS
D_EXPERT = D_MODEL // 2
BLOCK = 128
LANES = 128
NORM_ROWS = 256
NORM_EPS = 1e-6
NEG_INF = -1e30

A_QBLOCK = 2048
MOE_SUB = 256
MOE_CHUNK_SUBS = 5
MOE_TF = 256
MOE_DOWN_COLS = 512
MOE_ROW_GROUP = 512
GATHER_AHEAD = 2
VMEM_LIMIT = 56 * 1024 * 1024


def _cparams(n_axes, vmem=VMEM_LIMIT):
    return pltpu.CompilerParams(dimension_semantics=("arbitrary",) * n_axes, vmem_limit_bytes=vmem)


def _mod_kernel(c_ref, w_ref, b_ref, o_ref):
    c = c_ref[...]
    h = (c * jax.nn.sigmoid(c)).astype(BF16)
    o_ref[...] = jnp.dot(h, w_ref[...].astype(BF16), preferred_element_type=F32) + b_ref[...]


def _modulation(c_all, w, b, tn=1024):
    n_layers, d, n = w.shape
    r = c_all.shape[0]
    return pl.pallas_call(
        _mod_kernel,
        grid=(n_layers, n // tn),
        in_specs=[
            pl.BlockSpec((r, d), lambda l, j: (0, 0)),
            pl.BlockSpec((None, d, tn), lambda l, j: (l, 0, j)),
            pl.BlockSpec((None, 1, tn), lambda l, j: (l, 0, j)),
        ],
        out_specs=pl.BlockSpec((None, r, tn), lambda l, j: (l, 0, j)),
        out_shape=jax.ShapeDtypeStruct((n_layers, r, n), F32),
        compiler_params=_cparams(2),
        name="modulation",
    )(c_all, w, b.reshape(n_layers, 1, n))


class _Mod:
    def __init__(self, mod, n_sample, per_row, rows_per_seq=None):
        self.per_row = per_row
        self.n_sample = n_sample
        self.rows_per_seq = rows_per_seq
        n_layers, r, n = mod.shape
        self.arr = mod if per_row else mod.reshape(n_layers, r, 1, n)

    def spec(self, layer, chunk, width, tm, col_from_j):
        per = D_MODEL // width

        def col(rest):
            return chunk * per + (rest[0] if col_from_j else 0)

        if self.per_row:
            return pl.BlockSpec((None, self.n_sample, width), lambda i, *rest: (layer, 0, col(rest)))
        tiles_per_seq = self.rows_per_seq // tm
        base = self.n_sample
        return pl.BlockSpec((None, None, 1, width),
                            lambda i, *rest: (layer, base + i // tiles_per_seq, 0, col(rest)))


def _rope_tables(pos, head_dim):
    rot = head_dim // 4
    half = rot // 2
    inv = jnp.exp(jnp.arange(half, dtype=F32) * (-math.log(ROPE_THETA) / half))
    ang = pos.astype(F32)[:, None] * inv[None, :]
    cos, sin = jnp.cos(ang), jnp.sin(ang)
    lane = jnp.arange(LANES) % head_dim
    idx = lane % half
    c = jnp.where(lane[None, :] < rot, cos[:, idx], 1.0)
    s1 = jnp.where(((lane >= half) & (lane < rot))[None, :], sin[:, idx], 0.0)
    s2 = jnp.where((lane < half)[None, :], -sin[:, idx], 0.0)
    return c.astype(F32), s1.astype(F32), s2.astype(F32)


def _adaln(x, shift, scale):
    r = lax.rsqrt(jnp.mean(x * x, axis=-1, keepdims=True) + NORM_EPS)
    return x * r * (1.0 + scale) + shift


def _adaln_mm_kernel(x_ref, sh_ref, sc_ref, w_ref, g_ref, c_ref, s1_ref, s2_ref, o_ref, h_ref,
                     *, head_dim, n_norm, n_tiles, tn, row_groups):
    j = pl.program_id(1)

    @pl.when(j == 0)
    def _():
        h_ref[...] = _adaln(x_ref[...], sh_ref[...], sc_ref[...]).astype(BF16)

    def plain():
        o_ref[...] = jnp.dot(h_ref[...], w_ref[...].astype(BF16), preferred_element_type=F32)

    def normed():
        w16 = w_ref[...].astype(BF16)
        rows_per = h_ref.shape[0] // row_groups
        for rg in range(row_groups):
            rows = slice(rg * rows_per, (rg + 1) * rows_per)
            acc = jnp.dot(h_ref[rows, :], w16, preferred_element_type=F32)
            normed_rows(acc, rows)

    def normed_rows(acc, rows):
        half = head_dim // 8
        gain = g_ref[...]
        c, s1, s2 = c_ref[rows, :], s1_ref[rows, :], s2_ref[rows, :]
        for cb in range(tn // LANES):
            a = acc[:, cb * LANES:(cb + 1) * LANES]
            sq = a * a
            if head_dim == LANES:
                ms = jnp.mean(sq, axis=-1, keepdims=True)
            else:
                lo = lax.broadcasted_iota(jnp.int32, sq.shape, 1) < head_dim
                s_lo = jnp.sum(jnp.where(lo, sq, 0.0), axis=-1, keepdims=True)
                s_hi = jnp.sum(jnp.where(lo, 0.0, sq), axis=-1, keepdims=True)
                ms = jnp.where(lo, s_lo, s_hi) * (1.0 / head_dim)
            a = a * lax.rsqrt(ms + NORM_EPS) * gain
            a = a * c + pltpu.roll(a, half, 1) * s1 + pltpu.roll(a, LANES - half, 1) * s2
            o_ref[rows, cb * LANES:(cb + 1) * LANES] = a

    if n_norm >= n_tiles:
        normed()
    elif n_norm == 0:
        plain()
    else:
        pl.when(j < n_norm)(normed)
        pl.when(j >= n_norm)(plain)


def _adaln_matmul(x, mod, layer, chunks, w, w_layer, gains, tiles_per_gain, n_norm, head_dim, tables,
                  tm, tn, name):
    m, d = x.shape
    n = w.shape[-1]
    n_tiles = n // tn
    table_rows = tables[0].shape[0]
    table_tiles = table_rows // tm
    n_gains = gains.shape[0]
    tab_spec = pl.BlockSpec((tm, LANES), lambda i, j: (i % table_tiles, 0))
    kern = functools.partial(_adaln_mm_kernel, head_dim=head_dim, n_norm=n_norm, n_tiles=n_tiles, tn=tn,
                             row_groups=max(1, tm // NORM_ROWS))
    return pl.pallas_call(
        kern,
        grid=(m // tm, n_tiles),
        in_specs=[
            pl.BlockSpec((tm, d), lambda i, j: (i, 0)),
            mod.spec(layer, chunks[0], D_MODEL, tm, False),
            mod.spec(layer, chunks[1], D_MODEL, tm, False),
            pl.BlockSpec((None, d, tn), lambda i, j: (w_layer, 0, j)),
            pl.BlockSpec((None, 1, LANES), lambda i, j: (jnp.minimum(j // tiles_per_gain, n_gains - 1), 0, 0)),
            tab_spec, tab_spec, tab_spec,
        ],
        out_specs=pl.BlockSpec((tm, tn), lambda i, j: (i, j)),
        out_shape=jax.ShapeDtypeStruct((m, n), F32),
        scratch_shapes=[pltpu.VMEM((tm, d), BF16)],
        compiler_params=_cparams(2),
        name=name,
    )(x, mod.arr, mod.arr, w, gains, *tables)


def _oproj_kernel(o_ref, w_ref, x_ref, g_ref, out_ref):
    acc = jnp.dot(o_ref[...].astype(BF16), w_ref[...].astype(BF16), preferred_element_type=F32)
    out_ref[...] = x_ref[...] + g_ref[...] * acc


def _out_proj(o, w, w_layer, x, mod, layer, gate_chunk, tm, tn, name):
    m, d = x.shape
    k = o.shape[1]
    return pl.pallas_call(
        _oproj_kernel,
        grid=(m // tm, d // tn),
        in_specs=[
            pl.BlockSpec((tm, k), lambda i, j: (i, 0)),
            pl.BlockSpec((None, k, tn), lambda i, j: (w_layer, 0, j)),
            pl.BlockSpec((tm, tn), lambda i, j: (i, j)),
            mod.spec(layer, gate_chunk, tn, tm, True),
        ],
        out_specs=pl.BlockSpec((tm, tn), lambda i, j: (i, j)),
        out_shape=jax.ShapeDtypeStruct((m, d), F32),
        compiler_params=_cparams(2),
        name=name,
    )(o, w, x, mod.arr)


def _attn_a_prompt_kernel(q_ref, kp_ref, kc_ref, vp_ref, vc_ref, o_ref, kk, vv, ob, lb):
    first = pl.program_id(2) == 0
    qb = A_QBLOCK
    kk[0:qb, :] = kp_ref[...]
    kk[qb:2 * qb, :] = kc_ref[...]
    vv[0:qb, :] = vp_ref[...]
    vv[qb:2 * qb, :] = vc_ref[...]
    scale = A_HEAD_DIM ** -0.5
    qi = lax.broadcasted_iota(jnp.int32, (BLOCK, 2 * BLOCK), 0) + BLOCK
    kj = lax.broadcasted_iota(jnp.int32, (BLOCK, 2 * BLOCK), 1)
    dist = qi - kj
    band = (dist >= 0) & (dist <= BLOCK)
    band_first = band & (kj >= jnp.where(first, BLOCK, 0))
    for b, r in enumerate(A_BRANCH_DILATIONS):
        for rho in range(r):
            for m in range(qb // (BLOCK * r)):
                q0 = rho + BLOCK * r * m
                k0 = qb - BLOCK * r + q0
                if r == 1:
                    qsl, ksl = pl.ds(q0, BLOCK), pl.ds(k0, 2 * BLOCK)
                else:
                    qsl, ksl = pl.ds(q0, BLOCK, stride=r), pl.ds(k0, 2 * BLOCK, stride=r)
                q = q_ref[qsl, :].astype(BF16)
                k = kk[ksl, :].astype(BF16)
                v = vv[ksl, :].astype(BF16)
                s = lax.dot_general(q, k, (((1,), (1,)), ((), ())), preferred_element_type=F32) * scale
                mask = band_first if m == 0 else band
                s = jnp.where(mask, s, NEG_INF)
                mx = jnp.max(s, axis=-1, keepdims=True)
                p = jnp.exp(s - mx)
                den = jnp.sum(p, axis=-1, keepdims=True)
                o = jnp.dot(p.astype(BF16), v, preferred_element_type=F32) / den
                ob[b, qsl, :] = o
                lb[b, qsl, :] = jnp.broadcast_to(mx + jnp.log(den), (BLOCK, LANES))
    lse = [lb[b] for b in range(3)]
    top = jnp.maximum(jnp.maximum(lse[0], lse[1]), lse[2])
    w = [jnp.exp(l - top) for l in lse]
    tot = w[0] + w[1] + w[2]
    o_ref[...] = ((w[0] * ob[0] + w[1] * ob[1] + w[2] * ob[2]) / tot).astype(o_ref.dtype)


def _attn_a_prompt(qkv, n_seq, seq_len):
    h = A_HEADS
    qb = A_QBLOCK
    qkv3 = qkv.reshape(n_seq, seq_len, 3 * h * A_HEAD_DIM)

    def blk(col0, prev):
        if prev:
            return pl.BlockSpec((None, qb, LANES), lambda n, hh, t: (n, jnp.maximum(t - 1, 0), col0 + hh))
        return pl.BlockSpec((None, qb, LANES), lambda n, hh, t: (n, t, col0 + hh))

    out = pl.pallas_call(
        _attn_a_prompt_kernel,
        grid=(n_seq, h, seq_len // qb),
        in_specs=[blk(0, False), blk(h, True), blk(h, False), blk(2 * h, True), blk(2 * h, False)],
        out_specs=pl.BlockSpec((None, qb, LANES), lambda n, hh, t: (n, t, hh)),
        out_shape=jax.ShapeDtypeStruct((n_seq, seq_len, h * A_HEAD_DIM), BF16),
        scratch_shapes=[
            pltpu.VMEM((2 * qb, LANES), F32), pltpu.VMEM((2 * qb, LANES), F32),
            pltpu.VMEM((3, qb, LANES), F32), pltpu.VMEM((3, qb, LANES), F32),
        ],
        compiler_params=_cparams(3),
        name="attn_a_prompt",
    )(qkv3, qkv3, qkv3, qkv3, qkv3)
    return out.reshape(n_seq * seq_len, h * A_HEAD_DIM)


def _attn_b_prompt_kernel(sink_ref, q_ref, kvp_ref, kvc_ref, o_ref, kvs, *, tq):
    first = pl.program_id(1) == 0
    kv_w = 2 * B_KV_HEADS * B_HEAD_DIM
    k_cols = B_KV_HEADS * B_HEAD_DIM
    kvs[0:BLOCK, :] = kvp_ref[...]
    kvs[BLOCK:BLOCK + tq, :] = kvc_ref[...]
    scale = B_HEAD_DIM ** -0.5
    qi = lax.broadcasted_iota(jnp.int32, (BLOCK, 2 * BLOCK), 0) + BLOCK
    kj = lax.broadcasted_iota(jnp.int32, (BLOCK, 2 * BLOCK), 1)
    dist = qi - kj
    band = (dist >= 0) & (dist <= BLOCK)
    lane_half = lax.broadcasted_iota(jnp.int32, (2 * BLOCK, LANES), 1) // B_HEAD_DIM

    def sub_block(sb, carry):
        row0 = pl.multiple_of(sb * BLOCK, BLOCK)
        mask = band & (kj >= jnp.where(jnp.logical_and(first, sb == 0), BLOCK, 0))
        for hk in range(B_KV_HEADS):
            cbk, hh = hk // 2, hk % 2
            kblk = kvs[pl.ds(row0, 2 * BLOCK), cbk * LANES:(cbk + 1) * LANES]
            vblk = kvs[pl.ds(row0, 2 * BLOCK), k_cols + cbk * LANES:k_cols + (cbk + 1) * LANES]
            k_half, v_half = [], []
            for a in range(2):
                ka = kblk if a == hh else pltpu.roll(kblk, B_HEAD_DIM, 1)
                va = vblk if a == hh else pltpu.roll(vblk, B_HEAD_DIM, 1)
                k_half.append(jnp.where(lane_half == a, ka, 0.0).astype(BF16))
                v_half.append(jnp.where(lane_half == a, va, 0.0).astype(BF16))
            for c in range(hk * (B_GROUP // 2), (hk + 1) * (B_GROUP // 2)):
                q2 = q_ref[pl.ds(row0, BLOCK), c * LANES:(c + 1) * LANES].astype(BF16)
                o_pair = jnp.zeros((BLOCK, LANES), F32)
                for a in range(2):
                    sink = sink_ref[2 * c + a]
                    s = lax.dot_general(q2, k_half[a], (((1,), (1,)), ((), ())),
                                        preferred_element_type=F32) * scale
                    s = jnp.where(mask, s, NEG_INF)
                    mx = jnp.maximum(jnp.max(s, axis=-1, keepdims=True), sink)
                    p = jnp.exp(s - mx)
                    den = jnp.sum(p, axis=-1, keepdims=True) + jnp.exp(sink - mx)
                    o_pair = o_pair + jnp.dot(p.astype(BF16), v_half[a], preferred_element_type=F32) / den
                o_ref[pl.ds(row0, BLOCK), c * LANES:(c + 1) * LANES] = o_pair.astype(o_ref.dtype)
        return carry

    lax.fori_loop(0, tq // BLOCK, sub_block, 0)


def _attn_b_prompt(q, kv, sinks, n_seq, seq_len, tq=512):
    d = q.shape[1]
    kv_w = kv.shape[1]
    q3 = q.reshape(n_seq, seq_len, d)
    kv3 = kv.reshape(n_seq, seq_len, kv_w)
    per = tq // BLOCK
    out = pl.pallas_call(
        functools.partial(_attn_b_prompt_kernel, tq=tq),
        grid_spec=pltpu.PrefetchScalarGridSpec(
            num_scalar_prefetch=0,
            grid=(n_seq, seq_len // tq),
            in_specs=[
                pl.BlockSpec(memory_space=pltpu.SMEM),
                pl.BlockSpec((None, tq, d), lambda n, t: (n, t, 0)),
                pl.BlockSpec((None, BLOCK, kv_w), lambda n, t: (n, jnp.maximum(t * per - 1, 0), 0)),
                pl.BlockSpec((None, tq, kv_w), lambda n, t: (n, t, 0)),
            ],
            out_specs=pl.BlockSpec((None, tq, d), lambda n, t: (n, t, 0)),
            scratch_shapes=[pltpu.VMEM((BLOCK + tq, kv_w), F32)],
        ),
        out_shape=jax.ShapeDtypeStruct((n_seq, seq_len, d), BF16),
        compiler_params=_cparams(2),
        name="attn_b_prompt",
    )(sinks, q3, kv3, kv3)
    return out.reshape(n_seq * seq_len, d)


def _attn_a_step_kernel(qkv_ref, k1_ref, k4_ref, k16_ref, v1_ref, v4_ref, v16_ref, o_ref):
    scale = A_HEAD_DIM ** -0.5
    n_br = len(A_BRANCH_DILATIONS)
    q = qkv_ref[0:A_HEADS, :]
    k_new = qkv_ref[A_HEADS:2 * A_HEADS, :]
    v_new = qkv_ref[2 * A_HEADS:3 * A_HEADS, :]
    s_new = jnp.sum(q * k_new, axis=-1, keepdims=True) * scale
    s = [jnp.sum(k_ref[...] * q[None], axis=-1, keepdims=True) * scale
         for k_ref in (k1_ref, k4_ref, k16_ref)]
    top = s_new
    for sb in s:
        top = jnp.maximum(top, jnp.max(sb, axis=0))
    p_new = jnp.exp(s_new - top)
    den = n_br * p_new
    acc = (n_br * p_new) * v_new
    for sb, v_ref in zip(s, (v1_ref, v4_ref, v16_ref)):
        p = jnp.exp(sb - top[None])
        den = den + jnp.sum(p, axis=0)
        acc = acc + jnp.sum(p * v_ref[...], axis=0)
    o_ref[...] = (acc / den).astype(o_ref.dtype)


def _attn_a_step(qkv, cache_k, cache_v, layer):
    n = qkv.shape[0]
    d = A_HEADS * A_HEAD_DIM
    n_layers, _, buf = cache_k.shape[:3]
    assert buf == BLOCK * A_BRANCH_DILATIONS[-1]
    qkv3 = qkv.reshape(n, 3 * A_HEADS, A_HEAD_DIM)

    views, specs = [], []
    for cache in (cache_k, cache_v):
        for r in A_BRANCH_DILATIONS:
            views.append(cache.reshape(n_layers, n, buf // r, r, A_HEADS, A_HEAD_DIM))
            last = buf // (r * BLOCK) - 1
            specs.append(pl.BlockSpec((None, None, BLOCK, None, A_HEADS, A_HEAD_DIM),
                                      lambda i, last=last: (layer, i, last, 0, 0, 0)))
    out = pl.pallas_call(
        _attn_a_step_kernel,
        grid=(n,),
        in_specs=[pl.BlockSpec((None, 3 * A_HEADS, A_HEAD_DIM), lambda i: (i, 0, 0))] + specs,
        out_specs=pl.BlockSpec((None, A_HEADS, A_HEAD_DIM), lambda i: (i, 0, 0)),
        out_shape=jax.ShapeDtypeStruct((n, A_HEADS, A_HEAD_DIM), BF16),
        compiler_params=_cparams(1),
        name="attn_a_step",
    )(qkv3, *views)
    return out.reshape(n, d)


def _attn_b_step_kernel(q_ref, kvn_ref, ck_ref, cv_ref, sink_ref, o_ref):
    scale = B_HEAD_DIM ** -0.5
    for hk in range(B_KV_HEADS):
        rows = slice(hk * B_GROUP, (hk + 1) * B_GROUP)
        qg = q_ref[rows, :]
        k = ck_ref[:, hk, :]
        v = cv_ref[:, hk, :]
        k_new = kvn_ref[hk:hk + 1, :]
        v_new = kvn_ref[B_KV_HEADS + hk:B_KV_HEADS + hk + 1, :]
        sink = sink_ref[rows, :]
        s = lax.dot_general(qg.astype(BF16), k.astype(BF16), (((1,), (1,)), ((), ())),
                            preferred_element_type=F32) * scale
        s_new = jnp.sum(qg * k_new, axis=-1, keepdims=True) * scale
        top = jnp.maximum(jnp.maximum(jnp.max(s, axis=-1, keepdims=True), s_new), sink)
        p = jnp.exp(s - top)
        p_new = jnp.exp(s_new - top)
        den = jnp.sum(p, axis=-1, keepdims=True) + p_new + jnp.exp(sink - top)
        acc = jnp.dot(p.astype(BF16), v.astype(BF16), preferred_element_type=F32) + p_new * v_new
        o_ref[rows, :] = acc / den


def _attn_b_step(q, kv_new, cache_k, cache_v, sinks):
    n = q.shape[0]
    win = cache_k.shape[1]
    assert win == BLOCK
    q3 = q.reshape(n, B_HEADS, B_HEAD_DIM)
    kvn = kv_new.reshape(n, 2 * B_KV_HEADS, B_HEAD_DIM)
    cspec = pl.BlockSpec((None, win, B_KV_HEADS, B_HEAD_DIM), lambda i: (i, 0, 0, 0))
    out = pl.pallas_call(
        _attn_b_step_kernel,
        grid=(n,),
        in_specs=[
            pl.BlockSpec((None, B_HEADS, B_HEAD_DIM), lambda i: (i, 0, 0)),
            pl.BlockSpec((None, 2 * B_KV_HEADS, B_HEAD_DIM), lambda i: (i, 0, 0)),
            cspec, cspec,
            pl.BlockSpec((B_HEADS, 1), lambda i: (0, 0)),
        ],
        out_specs=pl.BlockSpec((None, B_HEADS, B_HEAD_DIM), lambda i: (i, 0, 0)),
        out_shape=jax.ShapeDtypeStruct((n, B_HEADS, B_HEAD_DIM), F32),
        compiler_params=_cparams(1),
        name="attn_b_step",
    )(q3, kvn, cache_k, cache_v, sinks.reshape(B_HEADS, 1))
    return out.reshape(n, B_HEADS * B_HEAD_DIM)


def _shift_b_kernel(kvn_ref, ck_ref, cv_ref, nk_ref, nv_ref):
    win = ck_ref.shape[0]
    nk_ref[0:win - 1] = ck_ref[1:win]
    nv_ref[0:win - 1] = cv_ref[1:win]
    nk_ref[win - 1] = kvn_ref[0:B_KV_HEADS, :]
    nv_ref[win - 1] = kvn_ref[B_KV_HEADS:2 * B_KV_HEADS, :]


def _shift_b_cache(kv_new, cache_k, cache_v):
    n, win = cache_k.shape[:2]
    kvn = kv_new.reshape(n, 2 * B_KV_HEADS, B_HEAD_DIM)
    cspec = pl.BlockSpec((None, win, B_KV_HEADS, B_HEAD_DIM), lambda i: (i, 0, 0, 0))
    return pl.pallas_call(
        _shift_b_kernel,
        grid=(n,),
        in_specs=[pl.BlockSpec((None, 2 * B_KV_HEADS, B_HEAD_DIM), lambda i: (i, 0, 0)), cspec, cspec],
        out_specs=[cspec, cspec],
        out_shape=[jax.ShapeDtypeStruct(cache_k.shape, F32), jax.ShapeDtypeStruct(cache_v.shape, F32)],
        compiler_params=_cparams(1),
        name="shift_b_cache",
    )(kvn, cache_k, cache_v)


SHIFT_ROWS = 512


def _shift_a_kernel(kc_ref, kx_ref, kn_ref, vc_ref, vx_ref, vn_ref, nk_ref, nv_ref):
    t = pl.program_id(2)
    last = pl.num_programs(2) - 1
    rows = nk_ref.shape[0]
    for cur, nxt, new, out in ((kc_ref, kx_ref, kn_ref, nk_ref), (vc_ref, vx_ref, vn_ref, nv_ref)):
        out[0:rows - 1] = cur[1:rows]

        @pl.when(t < last)
        def _(nxt=nxt, out=out):
            out[rows - 1] = nxt[0]

        @pl.when(t == last)
        def _(new=new, out=out):
            out[rows - 1] = new[0]


def _shift_a_cache(qkv_layers, cache_k, cache_v):
    n_layers, n_seq, buf = cache_k.shape[:3]
    d = A_HEADS * A_HEAD_DIM
    rows = SHIFT_ROWS
    k_new = jnp.stack([q[:, d:2 * d] for q in qkv_layers]).reshape(n_layers, n_seq, 1, A_HEADS, A_HEAD_DIM)
    v_new = jnp.stack([q[:, 2 * d:] for q in qkv_layers]).reshape(n_layers, n_seq, 1, A_HEADS, A_HEAD_DIM)
    cur = pl.BlockSpec((None, None, rows, A_HEADS, A_HEAD_DIM), lambda l, n, t: (l, n, t, 0, 0))
    nxt = pl.BlockSpec((None, None, 1, A_HEADS, A_HEAD_DIM),
                       lambda l, n, t: (l, n, jnp.minimum((t + 1) * rows, buf - 1), 0, 0))
    new = pl.BlockSpec((None, None, 1, A_HEADS, A_HEAD_DIM), lambda l, n, t: (l, n, 0, 0, 0))
    return pl.pallas_call(
        _shift_a_kernel,
        grid=(n_layers, n_seq, buf // rows),
        in_specs=[cur, nxt, new, cur, nxt, new],
        out_specs=[cur, cur],
        out_shape=[jax.ShapeDtypeStruct(cache_k.shape, F32), jax.ShapeDtypeStruct(cache_v.shape, F32)],
        compiler_params=_cparams(3),
        name="shift_a_cache",
    )(cache_k, cache_k, k_new, cache_v, cache_v, v_new)


def _route_tile(x_ref, sh_ref, sc_ref, rw_ref, rb_ref, h_ref, e_ref, w_ref, r_ref, cnt_ref):
    h = _adaln(x_ref[...], sh_ref[...], sc_ref[...])
    h_ref[...] = h
    logits = lax.dot_general(rw_ref[...], h, (((1,), (1,)), ((), ())),
                             precision=lax.Precision.HIGHEST, preferred_element_type=F32)
    scores = jax.nn.sigmoid(logits)
    sel = scores + rb_ref[...]
    tm = sel.shape[1]
    pos = lax.broadcasted_iota(jnp.int32, (EXPERTS_PER_GROUP, tm), 0)

    def top2(v):
        m1 = jnp.max(v, axis=0, keepdims=True)
        i1 = jnp.min(jnp.where(v == m1, pos, EXPERTS_PER_GROUP), axis=0, keepdims=True)
        v2 = jnp.where(pos == i1, -jnp.inf, v)
        m2 = jnp.max(v2, axis=0, keepdims=True)
        i2 = jnp.min(jnp.where(v2 == m2, pos, EXPERTS_PER_GROUP), axis=0, keepdims=True)
        return m1 + m2, i1, i2

    best, e1, e2 = None, None, None
    for g in range(N_GROUPS):
        gs, i1, i2 = top2(sel[g * EXPERTS_PER_GROUP:(g + 1) * EXPERTS_PER_GROUP, :])
        i1 = i1 + g * EXPERTS_PER_GROUP
        i2 = i2 + g * EXPERTS_PER_GROUP
        if g == 0:
            best, e1, e2 = gs, i1, i2
        else:
            take = gs > best
            best = jnp.where(take, gs, best)
            e1 = jnp.where(take, i1, e1)
            e2 = jnp.where(take, i2, e2)
    eid = lax.broadcasted_iota(jnp.int32, (N_EXPERTS, tm), 0)
    w1 = jnp.sum(jnp.where(eid == e1, scores, 0.0), axis=0, keepdims=True)
    w2 = jnp.sum(jnp.where(eid == e2, scores, 0.0), axis=0, keepdims=True)
    tot = w1 + w2
    e_ref[0:1, :] = e1
    e_ref[1:2, :] = e2
    w_ref[0:1, :] = w1 / tot
    w_ref[1:2, :] = w2 / tot
    own1, own2 = eid == e1, eid == e2
    earlier = (lax.broadcasted_iota(jnp.int32, (tm, tm), 0)
               < lax.broadcasted_iota(jnp.int32, (tm, tm), 1)).astype(BF16)
    pre1 = jnp.dot(own1.astype(BF16), earlier, preferred_element_type=F32)
    pre2 = jnp.dot(own2.astype(BF16), earlier, preferred_element_type=F32)
    n1 = jnp.sum(own1.astype(F32), axis=1, keepdims=True)
    n2 = jnp.sum(own2.astype(F32), axis=1, keepdims=True)
    base = cnt_ref[...]
    r1 = jnp.sum(jnp.where(own1, pre1 + base, 0.0), axis=0, keepdims=True)
    r2 = jnp.sum(jnp.where(own2, pre2 + (base + n1), 0.0), axis=0, keepdims=True)
    r_ref[0:1, :] = r1.astype(jnp.int32)
    r_ref[1:2, :] = r2.astype(jnp.int32)
    cnt_ref[...] = base + n1 + n2


def _router_kernel(*refs, n_real, aliased):
    ins, outs = (refs[:7], refs[7:]) if aliased else (refs[:6], refs[6:])
    x_ref, sh_ref, sc_ref, rw_ref, rb_ref, c0_ref = ins[:6]
    h_ref, e_ref, w_ref, r_ref, cnt_ref = outs
    i = pl.program_id(0)

    @pl.when(i == 0)
    def _():
        cnt_ref[...] = c0_ref[...]

    @pl.when(i < n_real)
    def _():
        _route_tile(x_ref, sh_ref, sc_ref, rw_ref, rb_ref, h_ref, e_ref, w_ref, r_ref, cnt_ref)

    @pl.when(i >= n_real)
    def _():
        h_ref[...] = jnp.zeros_like(h_ref)


def _router(x, mod, layer, router_w_t, router_bias, counts, tm, m_total, row0, h_all, name):
    m, d = x.shape
    blk0 = row0 // tm
    n_real = m // tm
    n_steps = n_real if h_all is not None else -(-m_total // tm)
    real = lambda i: jnp.minimum(i, n_real - 1)
    in_specs = [
        pl.BlockSpec((tm, d), lambda i: (real(i), 0)),
        mod.spec(layer, 3, D_MODEL, tm, False),
        mod.spec(layer, 4, D_MODEL, tm, False),
        pl.BlockSpec((N_EXPERTS, d), lambda i: (0, 0)),
        pl.BlockSpec((N_EXPERTS, 1), lambda i: (0, 0)),
        pl.BlockSpec((N_EXPERTS, 1), lambda i: (0, 0)),
    ]
    args = [x, mod.arr, mod.arr, router_w_t, router_bias.reshape(N_EXPERTS, 1), counts]
    if h_all is not None:
        in_specs.append(pl.BlockSpec(memory_space=pl.ANY))
        args.append(h_all)
    return pl.pallas_call(
        functools.partial(_router_kernel, n_real=n_real, aliased=h_all is not None),
        grid=(n_steps,),
        in_specs=in_specs,
        out_specs=[
            pl.BlockSpec((tm, d), lambda i: (blk0 + i, 0)),
            pl.BlockSpec((2, tm), lambda i: (0, real(i))),
            pl.BlockSpec((2, tm), lambda i: (0, real(i))),
            pl.BlockSpec((2, tm), lambda i: (0, real(i))),
            pl.BlockSpec((N_EXPERTS, 1), lambda i: (0, 0)),
        ],
        out_shape=[
            jax.ShapeDtypeStruct((m_total, d), F32),
            jax.ShapeDtypeStruct((2, m), jnp.int32),
            jax.ShapeDtypeStruct((2, m), F32),
            jax.ShapeDtypeStruct((2, m), jnp.int32),
            jax.ShapeDtypeStruct((N_EXPERTS, 1), F32),
        ],
        input_output_aliases={} if h_all is None else {len(args) - 1: 0},
        compiler_params=_cparams(1),
        name=name,
    )(*args)


def _row_gather(idx_of_row, src_ref, dst_ref, sem, n_rows):
    def copy(r):
        return pltpu.make_async_copy(src_ref.at[pl.ds(idx_of_row(r), 1)], dst_ref.at[pl.ds(r, 1)], sem)

    def start():
        def body(r, carry):
            copy(r).start()
            return carry
        lax.fori_loop(0, n_rows, body, 0, unroll=8)

    def wait():
        pltpu.make_async_copy(src_ref.at[pl.ds(0, n_rows)], dst_ref, sem).wait()

    return start, wait


def _gather_kernel(e_ref, r_ref, rs_ref, cnt_ref, vt_ref, nv_ref, h_ref, xs0_ref, xs_ref, buf, sem, src, *, m_t):
    del xs0_ref
    i = pl.program_id(0)
    nv = nv_ref[0]
    n_buf = GATHER_AHEAD + 1
    cur = i % n_buf

    def tile(t, b):
        base = vt_ref[t] * MOE_SUB
        return _row_gather(lambda r: src[base + r], h_ref, buf.at[b], sem.at[b], MOE_SUB)

    @pl.when(i == 0)
    def _():
        def clear(s, carry):
            src[s] = jnp.int32(0)
            return carry

        for e in range(N_EXPERTS):
            used = rs_ref[e] + cnt_ref[e]
            lax.fori_loop(used, rs_ref[e] + (cnt_ref[e] + MOE_SUB - 1) // MOE_SUB * MOE_SUB, clear, 0)
        for k in range(2):
            def fill(t, carry, k=k):
                p = k * m_t + t
                src[rs_ref[e_ref[p]] + r_ref[p]] = t
                return carry

            lax.fori_loop(0, m_t, fill, 0, unroll=8)
        for t in range(GATHER_AHEAD):
            @pl.when(t < nv)
            def _(t=t):
                tile(t, t)[0]()

    @pl.when(i < nv)
    def _():
        tile(i, cur)[1]()

        @pl.when(i + GATHER_AHEAD < nv)
        def _():
            tile(i + GATHER_AHEAD, (i + GATHER_AHEAD) % n_buf)[0]()

        xs_ref[...] = buf[cur].astype(BF16)


def _dispatch(e_flat, r_flat, region_start, counts, live_tiles, n_live, h_all, xs_init, max_live):
    m_t, d = h_all.shape
    n_slots = xs_init.shape[0]
    return pl.pallas_call(
        functools.partial(_gather_kernel, m_t=m_t),
        grid_spec=pltpu.PrefetchScalarGridSpec(
            num_scalar_prefetch=6,
            grid=(max_live,),
            in_specs=[pl.BlockSpec(memory_space=pl.ANY), pl.BlockSpec(memory_space=pl.ANY)],
            out_specs=pl.BlockSpec((MOE_SUB, d), lambda i, e, r, rs, cnt, vt, nv: (vt[jnp.minimum(i, nv[0] - 1)], 0)),
            scratch_shapes=[pltpu.VMEM((GATHER_AHEAD + 1, MOE_SUB, d), F32),
                            pltpu.SemaphoreType.DMA((GATHER_AHEAD + 1,)), pltpu.SMEM((n_slots,), jnp.int32)],
        ),
        out_shape=jax.ShapeDtypeStruct((n_slots, d), BF16),
        input_output_aliases={7: 0},
        compiler_params=_cparams(1),
        name="moe_dispatch",
    )(e_flat, r_flat, region_start, counts, live_tiles, n_live, h_all, xs_init)


def _experts_kernel(ce_ref, cs_ref, nu_ref, xs_ref, wg_ref, wu_ref, wd_ref, y_ref):
    c = pl.program_id(0)
    j = pl.program_id(1)

    @pl.when(c < nu_ref[0])
    def _():
        wg = wg_ref[...].astype(BF16)
        wu = wu_ref[...].astype(BF16)
        wd = wd_ref[...].astype(BF16)
        n_live = cs_ref[c]
        d = y_ref.shape[1]
        for n in range(1, MOE_CHUNK_SUBS + 1):
            for first in (True, False):
                @pl.when(jnp.logical_and(n_live == n, (j == 0) == first))
                def _(n=n, first=first):
                    for r0 in range(0, n * MOE_SUB, MOE_ROW_GROUP):
                        rows = slice(r0, min(r0 + MOE_ROW_GROUP, n * MOE_SUB))
                        x = xs_ref[rows, :]
                        a = jnp.dot(x, wg, preferred_element_type=F32)
                        u = jnp.dot(x, wu, preferred_element_type=F32)
                        hidden = (a * jax.nn.sigmoid(a) * u).astype(BF16)
                        for c0 in range(0, d, MOE_DOWN_COLS):
                            cols = slice(c0, c0 + MOE_DOWN_COLS)
                            part = jnp.dot(hidden, wd[:, cols], preferred_element_type=F32)
                            if first:
                                y_ref[rows, cols] = part
                            else:
                                y_ref[rows, cols] += part
                    if first and n * MOE_SUB < y_ref.shape[0]:
                        y_ref[n * MOE_SUB:, :] = jnp.zeros((y_ref.shape[0] - n * MOE_SUB, d), F32)

    @pl.when(jnp.logical_and(c >= nu_ref[0], j == 0))
    def _():
        y_ref[...] = jnp.zeros_like(y_ref)


def _experts(chunk_expert, chunk_subs, n_used, xs, w_gate, w_up, w_down, layer):
    n_slots, d = xs.shape
    f = w_gate.shape[-1]
    tf = MOE_TF
    nf = f // tf
    ch = MOE_SUB * MOE_CHUNK_SUBS

    def row(c, nu):
        return jnp.minimum(c, nu[0] - 1)

    def col(c, j, nu):
        return jnp.where(c < nu[0], j, nf - 1)

    return pl.pallas_call(
        _experts_kernel,
        grid_spec=pltpu.PrefetchScalarGridSpec(
            num_scalar_prefetch=3,
            grid=(n_slots // ch, nf),
            in_specs=[
                pl.BlockSpec((ch, d), lambda c, j, ce, cs, nu: (row(c, nu), 0)),
                pl.BlockSpec((None, None, d, tf), lambda c, j, ce, cs, nu: (layer, ce[c], 0, col(c, j, nu))),
                pl.BlockSpec((None, None, d, tf), lambda c, j, ce, cs, nu: (layer, ce[c], 0, col(c, j, nu))),
                pl.BlockSpec((None, None, tf, d), lambda c, j, ce, cs, nu: (layer, ce[c], col(c, j, nu), 0)),
            ],
            out_specs=pl.BlockSpec((ch, d), lambda c, j, ce, cs, nu: (c, 0)),
        ),
        out_shape=jax.ShapeDtypeStruct((n_slots, d), F32),
        compiler_params=_cparams(2),
        name="moe_experts",
    )(chunk_expert, chunk_subs, n_used, xs, w_gate, w_up, w_down)


def _combine_kernel(e_ref, r_ref, rs_ref, y_ref, x_ref, w_ref, g_ref, o_ref, buf, sem, *, tm, tok0, m_t):
    i = pl.program_id(0)
    n = pl.num_programs(0)
    cur = i % 2

    def tile(t, b, k):
        base = k * m_t + tok0 + t * tm
        return _row_gather(lambda r: rs_ref[e_ref[base + r]] + r_ref[base + r], y_ref, buf.at[b, k],
                           sem.at[b], tm)

    @pl.when(i == 0)
    def _():
        tile(0, 0, 0)[0]()
        tile(0, 0, 1)[0]()

    tile(i, cur, 0)[1]()
    tile(i, cur, 1)[1]()

    @pl.when(i + 1 < n)
    def _():
        tile(i + 1, 1 - cur, 0)[0]()
        tile(i + 1, 1 - cur, 1)[0]()

    w = w_ref[...]
    moe = w[:, 0:1] * buf[cur, 0] + w[:, 1:2] * buf[cur, 1]
    o_ref[...] = x_ref[...] + g_ref[...] * moe


def _combine(e_flat, r_flat, region_start, y, x, w_t, mod, layer, tok0, m_t, tm, name):
    m, d = x.shape
    return pl.pallas_call(
        functools.partial(_combine_kernel, tm=tm, tok0=tok0, m_t=m_t),
        grid_spec=pltpu.PrefetchScalarGridSpec(
            num_scalar_prefetch=3,
            grid=(m // tm,),
            in_specs=[
                pl.BlockSpec(memory_space=pl.ANY),
                pl.BlockSpec((tm, d), lambda i, *_: (i, 0)),
                pl.BlockSpec((tm, 2), lambda i, *_: (tok0 // tm + i, 0)),
                mod.spec(layer, 5, D_MODEL, tm, False),
            ],
            out_specs=pl.BlockSpec((tm, d), lambda i, *_: (i, 0)),
            scratch_shapes=[pltpu.VMEM((2, 2, tm, d), F32), pltpu.SemaphoreType.DMA((2,))],
        ),
        out_shape=jax.ShapeDtypeStruct((m, d), F32),
        compiler_params=_cparams(1),
        name=name,
    )(e_flat, r_flat, region_start, y, x, w_t, mod.arr)


def _slot_plan(counts, max_chunks, max_live):
    sub, per = MOE_SUB, MOE_CHUNK_SUBS
    n_sub = (counts + sub - 1) // sub
    n_chunk = (n_sub + per - 1) // per
    chunk_end = jnp.cumsum(n_chunk)
    chunk_base = chunk_end - n_chunk
    n_used = chunk_end[-1]
    c = jnp.minimum(jnp.arange(max_chunks, dtype=jnp.int32), n_used - 1)
    owner = lambda ends, i: jnp.minimum(jnp.sum(i[:, None] >= ends[None, :], axis=1), N_EXPERTS - 1)
    chunk_expert = owner(chunk_end, c)
    chunk_subs = jnp.clip(n_sub[chunk_expert] - (c - chunk_base[chunk_expert]) * per, 0, per)
    sub_end = jnp.cumsum(n_sub)
    n_live = sub_end[-1]
    t = jnp.minimum(jnp.arange(max_live, dtype=jnp.int32), n_live - 1)
    t_expert = owner(sub_end, t)
    live_tiles = chunk_base[t_expert] * per + (t - (sub_end - n_sub)[t_expert])
    i32 = lambda a: a.astype(jnp.int32)
    return (i32(chunk_base * (per * sub)), i32(live_tiles), i32(n_live).reshape(1), i32(chunk_expert),
            i32(chunk_subs), i32(n_used).reshape(1))


def _moe_layer(x_p, x_s, mod_p, mod_s, layer, router_w_t, router_bias, w_gate, w_up, w_down, xs_prev):
    m_p, d = x_p.shape
    m_s = x_s.shape[0]
    m_t = m_p + m_s
    sub, per = MOE_SUB, MOE_CHUNK_SUBS
    max_live = (2 * m_t) // sub + N_EXPERTS
    max_chunks = max_live // per + N_EXPERTS
    n_slots = max_chunks * per * sub
    if xs_prev is None:
        xs_prev = jnp.zeros((n_slots, d), BF16)
    zero = jnp.zeros((N_EXPERTS, 1), F32)
    h_all, e_p, w_p, r_p, cnt = _router(x_p, mod_p, layer, router_w_t, router_bias, zero, 512, m_t, 0, None,
                                        "router_prompt")
    h_all, e_s, w_s, r_s, cnt = _router(x_s, mod_s, layer, router_w_t, router_bias, cnt, m_s, m_t, m_p, h_all,
                                        "router_sample")
    e_flat = jnp.concatenate([e_p, e_s], axis=1).reshape(-1)
    r_flat = jnp.concatenate([r_p, r_s], axis=1).reshape(-1)
    w_t = jnp.concatenate([w_p, w_s], axis=1).T
    counts = cnt[:, 0].astype(jnp.int32)
    region_start, live_tiles, n_live, chunk_expert, chunk_subs, n_used = _slot_plan(counts, max_chunks, max_live)
    xs = _dispatch(e_flat, r_flat, region_start, counts, live_tiles, n_live, h_all, xs_prev, max_live)
    y = _experts(chunk_expert, chunk_subs, n_used, xs, w_gate, w_up, w_down, layer)
    x_p = _combine(e_flat, r_flat, region_start, y, x_p, w_t, mod_p, layer, 0, m_t, 256, "combine_prompt")
    x_s = _combine(e_flat, r_flat, region_start, y, x_s, w_t, mod_s, layer, m_p, m_t, m_s, "combine_sample")
    return x_p, x_s, xs


def kernel(x_prompt, x_sample, cache_a_k, cache_a_v, cache_b_k, cache_b_v, c_prompt, c_sample, a_w_qkv, a_q_gain, a_k_gain, a_w_o, b_w_q, b_q_gain, b_sinks, b_w_o, kv_w, kv_k_gain, kv_mod_w, kv_mod_b, mod_w, mod_b, router_w, router_bias, moe_w_gate, moe_w_up, moe_w_down):
    n_seq, seq_len, d = x_prompt.shape
    n_smp = x_sample.shape[0]
    m_p = n_seq * seq_len
    x_p = x_prompt.reshape(m_p, d)
    x_s = x_sample.reshape(n_smp, d)

    n_rows = -(-(n_smp + n_seq) // 8) * 8
    c_all = jnp.concatenate([c_sample, c_prompt, jnp.zeros((n_rows - n_smp - n_seq, d), F32)], axis=0)
    mod = _modulation(c_all, mod_w, mod_b)
    kv_mod = _modulation(c_all, kv_mod_w[None], kv_mod_b[None])
    mod_p = _Mod(mod, n_smp, False, seq_len)
    mod_s = _Mod(mod, n_smp, True)
    kvmod_p = _Mod(kv_mod, n_smp, False, seq_len)
    kvmod_s = _Mod(kv_mod, n_smp, True)

    pos_p = jnp.arange(seq_len, dtype=jnp.int32)
    pos_s = jnp.full((n_smp,), PAST_LEN, dtype=jnp.int32)
    tab_a_p, tab_a_s = _rope_tables(pos_p, A_HEAD_DIM), _rope_tables(pos_s, A_HEAD_DIM)
    tab_b_p, tab_b_s = _rope_tables(pos_p, B_HEAD_DIM), _rope_tables(pos_s, B_HEAD_DIM)

    router_w_t = router_w.T
    a_gains = jnp.stack([a_q_gain, a_k_gain], axis=1).reshape(N_A_LAYERS, 2, 1, A_HEAD_DIM)
    b_gains = jnp.concatenate([b_q_gain, b_q_gain], axis=-1).reshape(-1, 1, 1, LANES)
    kv_gain = jnp.concatenate([kv_k_gain, kv_k_gain]).reshape(1, 1, LANES)
    da = A_HEADS * A_HEAD_DIM

    tm_p, tn = 1024, 512
    qkv_s_layers, ak_p, av_p = [], [], []
    kv_p = kv_s = xs_buf = None
    for layer in range(DEPTH):
        if layer < N_A_LAYERS:
            qkv_p = _adaln_matmul(x_p, mod_p, layer, (0, 1), a_w_qkv, layer, a_gains[layer], da // tn,
                                  2 * da // tn, A_HEAD_DIM, tab_a_p, tm_p, tn, "qkv_prompt")
            qkv_s = _adaln_matmul(x_s, mod_s, layer, (0, 1), a_w_qkv, layer, a_gains[layer], da // tn,
                                  2 * da // tn, A_HEAD_DIM, tab_a_s, n_smp, tn, "qkv_sample")
            o_p = _attn_a_prompt(qkv_p, n_seq, seq_len)
            o_s = _attn_a_step(qkv_s, cache_a_k, cache_a_v, layer)
            qkv_s_layers.append(qkv_s)
            keep = min(BLOCK * A_BRANCH_DILATIONS[-1], seq_len)
            qkv3 = qkv_p.reshape(n_seq, seq_len, 3 * da)
            ak_p.append(qkv3[:, seq_len - keep:, da:2 * da].reshape(n_seq, keep, A_HEADS, A_HEAD_DIM))
            av_p.append(qkv3[:, seq_len - keep:, 2 * da:].reshape(n_seq, keep, A_HEADS, A_HEAD_DIM))
            w_o, w_o_layer = a_w_o, layer
        else:
            jb = layer - N_A_LAYERS
            if layer == N_A_LAYERS:
                kvn = 2 * B_KV_HEADS * B_HEAD_DIM
                kv_p = _adaln_matmul(x_p, kvmod_p, 0, (0, 1), kv_w[None], 0, kv_gain, 1, 1, B_HEAD_DIM,
                                     tab_b_p, tm_p, kvn // 2, "kv_prompt")
                kv_s = _adaln_matmul(x_s, kvmod_s, 0, (0, 1), kv_w[None], 0, kv_gain, 1, 1, B_HEAD_DIM,
                                     tab_b_s, n_smp, kvn // 2, "kv_sample")
            q_p = _adaln_matmul(x_p, mod_p, layer, (0, 1), b_w_q, jb, b_gains[jb], d // tn, d // tn,
                                B_HEAD_DIM, tab_b_p, tm_p, tn, "q_prompt")
            q_s = _adaln_matmul(x_s, mod_s, layer, (0, 1), b_w_q, jb, b_gains[jb], d // tn, d // tn,
                                B_HEAD_DIM, tab_b_s, n_smp, tn, "q_sample")
            o_p = _attn_b_prompt(q_p, kv_p, b_sinks[jb], n_seq, seq_len)
            o_s = _attn_b_step(q_s, kv_s, cache_b_k, cache_b_v, b_sinks[jb])
            w_o, w_o_layer = b_w_o, jb
        x_p = _out_proj(o_p, w_o, w_o_layer, x_p, mod_p, layer, 2, tm_p, tn, "oproj_prompt")
        x_s = _out_proj(o_s, w_o, w_o_layer, x_s, mod_s, layer, 2, n_smp, tn, "oproj_sample")
        x_p, x_s, xs_buf = _moe_layer(x_p, x_s, mod_p, mod_s, layer, router_w_t, router_bias,
                                      moe_w_gate, moe_w_up, moe_w_down, xs_buf)

    ak_s, av_s = _shift_a_cache(qkv_s_layers, cache_a_k, cache_a_v)
    bk_s, bv_s = _shift_b_cache(kv_s, cache_b_k, cache_b_v)
    keep_b = min(BLOCK, seq_len)
    kvh = B_KV_HEADS * B_HEAD_DIM
    kv_tail = kv_p.reshape(n_seq, seq_len, 2 * kvh)[:, seq_len - keep_b:]
    bk_p = kv_tail[..., :kvh].reshape(n_seq, keep_b, B_KV_HEADS, B_HEAD_DIM)
    bv_p = kv_tail[..., kvh:].reshape(n_seq, keep_b, B_KV_HEADS, B_HEAD_DIM)
    return (x_p.reshape(n_seq, seq_len, d), x_s.reshape(n_smp, 1, d),
            jnp.stack(ak_p), jnp.stack(av_p), bk_p, bv_p, ak_s, av_s, bk_s, bv_s)
```
